```python
import jax, jax.numpy as jnp
from jax import lax
import numpy as np

D_MODEL = 1024
BATCH = 1
SEQ = 16384
DEPTH = 1
DEC_BATCH = 32
DEC_SEQ = 1
PAST_LEN = 16384
PAGE_SIZE = 128

D_MIX = D_MODEL
D_GMLP = D_MIX // 2
D_ATTN = D_MIX - D_GMLP
CHUNK = 128
GMLP_GROUPS = 4
GMLP_GROUP_W = D_GMLP // GMLP_GROUPS
HEAD_DIM = 64
N_HEADS = D_ATTN // HEAD_DIM
IDX_HEADS = 8
IDX_DIM = 64
TOPK_MAX = 256
Q_BLOCK = 128
N_EXPERTS = 32
TOP_K_EXPERTS = 4
D_FF = D_MODEL
SWIGLU_LIMIT = 7.0
SWIGLU_ALPHA = 1.702
MOE_BLOCK = 128
EPS = 1e-6
SPLIT_SIZES = (D_GMLP, D_GMLP, D_ATTN, D_ATTN, D_ATTN, IDX_HEADS * IDX_DIM, IDX_DIM, IDX_HEADS)
SPLIT_POINTS = tuple(int(s) for s in np.cumsum(SPLIT_SIZES)[:-1])
D_IN = int(sum(SPLIT_SIZES))

kernel_name = "hybrid_gmlp_dsa_moe_step"


def rmsnorm(x, g):
    xf = x.astype(jnp.float32)
    y = xf * lax.rsqrt(jnp.mean(xf * xf, axis=-1, keepdims=True) + EPS)
    return (y * g.astype(jnp.float32)).astype(x.dtype)


def layernorm(x, g, b):
    xf = x.astype(jnp.float32)
    mu = jnp.mean(xf, axis=-1, keepdims=True)
    var = jnp.mean(jnp.square(xf - mu), axis=-1, keepdims=True)
    return ((xf - mu) * lax.rsqrt(var + EPS) * g.astype(jnp.float32) + b.astype(jnp.float32)).astype(x.dtype)


def adaln(c, w_ada, b_ada):
    mod = jax.nn.silu(c) @ w_ada + b_ada
    return jnp.split(mod[:, None, :], 6, axis=-1)


def modulate(h, shift, scale):
    return h * (1.0 + scale) + shift


def spatial_gate(u, v, w_s, b_s, chunk_len):
    B, T, _ = v.shape
    n = T // chunk_len
    vg = v.reshape(B, n, chunk_len, GMLP_GROUPS, GMLP_GROUP_W)
    mask = jnp.tril(jnp.ones((chunk_len, chunk_len), dtype=bool))
    w = jnp.where(mask[None], w_s[:, :chunk_len, :chunk_len], 0.0)
    s = jnp.einsum('gij,bnjgc->bnigc', w, vg) + jnp.transpose(b_s[:, :chunk_len])[None, None, :, :, None]
    return u * s.reshape(B, T, D_GMLP)


def indexer_scores(q_idx, k_idx, w_idx):
    dots = jnp.einsum('bqhd,bsd->bqhs', q_idx.astype(jnp.float32), k_idx.astype(jnp.float32)) * (IDX_DIM ** -0.5)
    w = w_idx.astype(jnp.float32) * (IDX_HEADS ** -0.5)
    return jnp.einsum('bqh,bqhs->bqs', w, jax.nn.relu(dots))


def sparse_attend(q, kg, vg, valid):
    logits = jnp.einsum('bqhd,bqkhd->bqhk', q.astype(jnp.float32), kg.astype(jnp.float32)) * (HEAD_DIM ** -0.5)
    logits = jnp.where(valid[:, :, None, :], logits, -jnp.inf)
    p = jax.nn.softmax(logits, axis=-1)
    return jnp.einsum('bqhk,bqkhd->bqhd', p, vg.astype(jnp.float32)).astype(q.dtype)


def gather_rows(a, idx):
    return jax.vmap(lambda aa, ii: aa[ii])(a, idx)


def dsa_prompt(q, k, v, q_idx, k_idx, w_idx):
    B, T = q.shape[:2]
    topk = min(TOPK_MAX, T // 4)
    key_pos = jnp.arange(T)

    def block(i):
        s0 = i * Q_BLOCK
        qb = lax.dynamic_slice_in_dim(q, s0, Q_BLOCK, axis=1)
        qib = lax.dynamic_slice_in_dim(q_idx, s0, Q_BLOCK, axis=1)
        wb = lax.dynamic_slice_in_dim(w_idx, s0, Q_BLOCK, axis=1)
        qpos = s0 + jnp.arange(Q_BLOCK)
        score = indexer_scores(qib, k_idx, wb)
        score = jnp.where((key_pos[None, :] <= qpos[:, None])[None], score, -jnp.inf)
        _, sel = lax.top_k(score, topk)
        valid = sel <= qpos[None, :, None]
        return sparse_attend(qb, gather_rows(k, sel), gather_rows(v, sel), valid)

    outs = lax.map(block, jnp.arange(T // Q_BLOCK))
    return jnp.transpose(outs, (1, 0, 2, 3, 4)).reshape(B, T, D_ATTN)


def dsa_sample(q, k, v, q_idx, k_idx, w_idx, cache_k, cache_v, cache_kidx, page_table):
    Bd, Tn = q.shape[:2]
    n_pages = page_table.shape[1]
    past = n_pages * PAGE_SIZE
    L = past + Tn
    topk = min(TOPK_MAX, L // 4)
    kidx_past = cache_kidx[page_table].reshape(Bd, past, IDX_DIM)
    kidx_all = jnp.concatenate([kidx_past, k_idx.astype(kidx_past.dtype)], axis=1)
    qpos = past + jnp.arange(Tn)
    score = indexer_scores(q_idx, kidx_all, w_idx)
    score = jnp.where((jnp.arange(L)[None, :] <= qpos[:, None])[None], score, -jnp.inf)
    _, sel = lax.top_k(score, topk)
    valid = sel <= qpos[None, :, None]
    in_past = sel < past
    sp = jnp.minimum(sel, past - 1)
    phys = jax.vmap(lambda pt, s: pt[s // PAGE_SIZE])(page_table, sp)
    row = sp % PAGE_SIZE
    sn = jnp.clip(sel - past, 0, Tn - 1)
    sel_mask = in_past[..., None, None]
    kg = jnp.where(sel_mask, cache_k[phys, row], gather_rows(k, sn))
    vg = jnp.where(sel_mask, cache_v[phys, row], gather_rows(v, sn))
    return sparse_attend(q, kg, vg, valid).reshape(Bd, Tn, D_ATTN)


def clamped_swiglu(gu):
    gate, up = jnp.split(gu, 2, axis=-1)
    gate = jnp.minimum(gate, SWIGLU_LIMIT)
    up = jnp.clip(up, -SWIGLU_LIMIT, SWIGLU_LIMIT)
    return (up + 1.0) * (gate * jax.nn.sigmoid(SWIGLU_ALPHA * gate))


def moe(h, w_router, b_router, w_gate_up, b_gate_up, w_down, b_down):
    B, T, D = h.shape
    x = h.reshape(-1, D)
    n_tok = x.shape[0]
    logits = x.astype(jnp.float32) @ w_router.astype(jnp.float32) + b_router.astype(jnp.float32)
    top_logit, top_e = lax.top_k(logits, TOP_K_EXPERTS)
    gates = jax.nn.softmax(top_logit, axis=-1)
    n_assign = n_tok * TOP_K_EXPERTS
    flat_e = top_e.reshape(-1)
    flat_tok = jnp.repeat(jnp.arange(n_tok, dtype=jnp.int32), TOP_K_EXPERTS)
    flat_w = gates.reshape(-1)
    order = jnp.argsort(flat_e)
    se, stok, sw = flat_e[order], flat_tok[order], flat_w[order]
    counts = jnp.zeros((N_EXPERTS,), jnp.int32).at[flat_e].add(1)
    padded = (counts + MOE_BLOCK - 1) // MOE_BLOCK * MOE_BLOCK
    start = jnp.cumsum(counts) - counts
    pad_end = jnp.cumsum(padded)
    pad_start = pad_end - padded
    dest = pad_start[se] + (jnp.arange(n_assign, dtype=jnp.int32) - start[se])
    n_blocks = -(-n_assign // MOE_BLOCK) + N_EXPERTS
    n_rows = n_blocks * MOE_BLOCK
    row_tok = jnp.full((n_rows,), n_tok, jnp.int32).at[dest].set(stok)
    row_w = jnp.zeros((n_rows,), jnp.float32).at[dest].set(sw)
    block_e = jnp.minimum(jnp.searchsorted(pad_end, jnp.arange(n_blocks, dtype=jnp.int32) * MOE_BLOCK, side='right'), N_EXPERTS - 1)
    x_pad = jnp.concatenate([x, jnp.zeros((1, D), x.dtype)], axis=0)

    def run_block(args):
        toks, e = args
        xb = x_pad[toks]
        a = clamped_swiglu(xb @ w_gate_up[e] + b_gate_up[e])
        return a @ w_down[e] + b_down[e]

    yb = lax.map(run_block, (row_tok.reshape(n_blocks, MOE_BLOCK), block_e))
    y = jnp.zeros((n_tok + 1, D), jnp.float32).at[row_tok].add(yb.reshape(n_rows, D).astype(jnp.float32) * row_w[:, None])
    return y[:n_tok].reshape(B, T, D).astype(h.dtype)


def layer_forward(x, c, attn_fn, chunk_len, w_ada, b_ada, g_pre_mix, w_in, gmlp_ln_g, gmlp_ln_b, gmlp_w_s, gmlp_b_s,
                  w_out, g_post_mix, g_pre_ffn, w_router, b_router, w_gate_up, b_gate_up, w_down, b_down, g_post_ffn):
    B, T, _ = x.shape
    shift1, scale1, gate1, shift2, scale2, gate2 = adaln(c, w_ada, b_ada)
    h = modulate(rmsnorm(x, g_pre_mix), shift1, scale1)
    u, v, q, k, vv, q_idx, k_idx, w_idx = jnp.split(h @ w_in, SPLIT_POINTS, axis=-1)
    v_n = layernorm(jax.nn.gelu(v), gmlp_ln_g, gmlp_ln_b)
    out_a = spatial_gate(jax.nn.gelu(u), v_n, gmlp_w_s, gmlp_b_s, chunk_len)
    q = q.reshape(B, T, N_HEADS, HEAD_DIM)
    k = k.reshape(B, T, N_HEADS, HEAD_DIM)
    vv = vv.reshape(B, T, N_HEADS, HEAD_DIM)
    q_idx = q_idx.reshape(B, T, IDX_HEADS, IDX_DIM)
    out_b = attn_fn(q, k, vv, q_idx, k_idx, w_idx)
    mix = jnp.concatenate([out_a, out_b], axis=-1) @ w_out
    x = x + gate1 * rmsnorm(mix, g_post_mix)
    h2 = modulate(rmsnorm(x, g_pre_ffn), shift2, scale2)
    f = moe(h2, w_router, b_router, w_gate_up, b_gate_up, w_down, b_down)
    x = x + gate2 * rmsnorm(f, g_post_ffn)
    return x, k, vv, k_idx, v_n


def setup_inputs(seed: int = 0) -> dict:
    key = jax.random.key(seed)
    ks = jax.random.split(key, 32)
    n_pages = PAST_LEN // PAGE_SIZE
    n_used = DEC_BATCH * n_pages
    n_phys = n_used + max(1, n_used // 4)
    f32 = jnp.float32

    def nrm(k, shape, scale):
        return jax.random.normal(k, shape, f32) * scale

    def gain(k, shape):
        return 1.0 + 0.02 * jax.random.normal(k, shape, f32)

    return {
        "x_prompt": nrm(ks[0], (BATCH, SEQ, D_MODEL), 1.0),
        "x_sample": nrm(ks[1], (DEC_BATCH, DEC_SEQ, D_MODEL), 1.0),
        "c_prompt": nrm(ks[2], (BATCH, D_MODEL), 1.0),
        "c_sample": nrm(ks[3], (DEC_BATCH, D_MODEL), 1.0),
        "cache_k": nrm(ks[4], (DEPTH, n_phys, PAGE_SIZE, N_HEADS, HEAD_DIM), 1.0),
        "cache_v": nrm(ks[5], (DEPTH, n_phys, PAGE_SIZE, N_HEADS, HEAD_DIM), 1.0),
        "cache_kidx": nrm(ks[6], (DEPTH, n_phys, PAGE_SIZE, IDX_DIM), 1.0),
        "page_table": jax.random.permutation(ks[7], n_phys)[:n_used].reshape(DEC_BATCH, n_pages).astype(jnp.int32),
        "w_ada": nrm(ks[8], (DEPTH, D_MODEL, 6 * D_MODEL), 0.5 * D_MODEL ** -0.5),
        "b_ada": nrm(ks[9], (DEPTH, 6 * D_MODEL), 0.01),
        "g_pre_mix": gain(ks[10], (DEPTH, D_MODEL)),
        "w_in": nrm(ks[11], (DEPTH, D_MODEL, D_IN), D_MODEL ** -0.5),
        "gmlp_ln_g": gain(ks[12], (DEPTH, D_GMLP)),
        "gmlp_ln_b": nrm(ks[13], (DEPTH, D_GMLP), 0.01),
        "gmlp_w_s": nrm(ks[14], (DEPTH, GMLP_GROUPS, CHUNK, CHUNK), CHUNK ** -0.5),
        "gmlp_b_s": gain(ks[15], (DEPTH, GMLP_GROUPS, CHUNK)),
        "w_out": nrm(ks[16], (DEPTH, D_MIX, D_MODEL), D_MIX ** -0.5),
        "g_post_mix": gain(ks[17], (DEPTH, D_MODEL)),
        "g_pre_ffn": gain(ks[18], (DEPTH, D_MODEL)),
        "w_router": nrm(ks[19], (DEPTH, D_MODEL, N_EXPERTS), D_MODEL ** -0.5),
        "b_router": nrm(ks[20], (DEPTH, N_EXPERTS), 0.01),
        "w_gate_up": nrm(ks[21], (DEPTH, N_EXPERTS, D_MODEL, 2 * D_FF), D_MODEL ** -0.5),
        "b_gate_up": nrm(ks[22], (DEPTH, N_EXPERTS, 2 * D_FF), 0.01),
        "w_down": nrm(ks[23], (DEPTH, N_EXPERTS, D_FF, D_MODEL), D_FF ** -0.5),
        "b_down": nrm(ks[24], (DEPTH, N_EXPERTS, D_MODEL), 0.01),
        "g_post_ffn": gain(ks[25], (DEPTH, D_MODEL)),
    }


def reference(x_prompt, x_sample, c_prompt, c_sample, cache_k, cache_v, cache_kidx, page_table,
              w_ada, b_ada, g_pre_mix, w_in, gmlp_ln_g, gmlp_ln_b, gmlp_w_s, gmlp_b_s, w_out, g_post_mix,
              g_pre_ffn, w_router, b_router, w_gate_up, b_gate_up, w_down, b_down, g_post_ffn):
    yp, ys = x_prompt, x_sample
    kp_l, vp_l, ip_l, ks_l, vs_l, is_l, gs_l = [], [], [], [], [], [], []
    for l in range(DEPTH):
        wts = (w_ada[l], b_ada[l], g_pre_mix[l], w_in[l], gmlp_ln_g[l], gmlp_ln_b[l], gmlp_w_s[l], gmlp_b_s[l],
               w_out[l], g_post_mix[l], g_pre_ffn[l], w_router[l], b_router[l], w_gate_up[l], b_gate_up[l],
               w_down[l], b_down[l], g_post_ffn[l])
        yp, kp, vp, ip, _ = layer_forward(yp, c_prompt, dsa_prompt, CHUNK, *wts)
        ck, cv, ci = cache_k[l], cache_v[l], cache_kidx[l]

        def attn_sample(q, k, v, q_idx, k_idx, w_idx, ck=ck, cv=cv, ci=ci):
            return dsa_sample(q, k, v, q_idx, k_idx, w_idx, ck, cv, ci, page_table)

        ys, kn, vn, inew, gv = layer_forward(ys, c_sample, attn_sample, x_sample.shape[1], *wts)
        kp_l.append(kp); vp_l.append(vp); ip_l.append(ip)
        ks_l.append(kn); vs_l.append(vn); is_l.append(inew); gs_l.append(gv)
    return (yp, ys, jnp.stack(kp_l), jnp.stack(vp_l), jnp.stack(ip_l), jnp.stack(ks_l), jnp.stack(vs_l), jnp.stack(is_l), jnp.stack(gs_l))
```

```python
import functools
import math

import jax
import jax.numpy as jnp
from jax import lax
from jax.experimental import pallas as pl
from jax.experimental.pallas import tpu as pltpu

F32 = jnp.float32
BF16 = jnp.bfloat16
I32 = jnp.int32

EPS = 1e-6
N_HEADS = 8
HEAD_DIM = 64
IDX_HEADS = 8
IDX_DIM = 64
GMLP_GROUPS = 4
CHUNK = 128
TOPK_MAX = 256
PAGE_SIZE = 128
N_EXPERTS = 32
TOP_K_EXPERTS = 4
SWIGLU_LIMIT = 7.0
SWIGLU_ALPHA = 1.702

LANES = 128
VMEM_LIMIT = 60000 * 1024
INT_MIN = -2147483648
INT_MAX = 2147483647
NEG_INF = float("-inf")


def _cparams(*sem):
    return pltpu.CompilerParams(dimension_semantics=sem, vmem_limit_bytes=VMEM_LIMIT)


def _gelu_tanh(x):
    return 0.5 * x * (1.0 + jnp.tanh(0.7978845608028654 * (x + 0.044715 * (x * x * x))))


def _rms(x, g):
    return x * lax.rsqrt(jnp.mean(x * x, axis=-1, keepdims=True) + EPS) * g


def _ordered_bits_to_float(u):
    bits = jnp.where(u >= 0, u, u ^ jnp.int32(0x7FFFFFFF))
    return lax.bitcast_convert_type(bits, F32)


def _adaln_kernel(c_ref, w_ref, b_ref, o_ref):
    c = c_ref[...]
    a = c * (1.0 / (1.0 + jnp.exp(-c)))
    o_ref[...] = jnp.dot(a.astype(BF16), w_ref[...].astype(BF16), preferred_element_type=F32) + b_ref[...]


def _adaln(c_all, w_ada, b_ada):
    m, d = c_all.shape
    n = w_ada.shape[1]
    tn = 1536
    return pl.pallas_call(
        _adaln_kernel,
        out_shape=jax.ShapeDtypeStruct((m, n), F32),
        grid=(n // tn,),
        in_specs=[pl.BlockSpec((m, d), lambda j: (0, 0)),
                  pl.BlockSpec((d, tn), lambda j: (0, j)),
                  pl.BlockSpec((1, tn), lambda j: (0, j))],
        out_specs=pl.BlockSpec((m, tn), lambda j: (0, j)),
        compiler_params=_cparams("arbitrary"),
        name="adaln",
    )(c_all, w_ada, b_ada.reshape(1, n))


_C_U, _C_V, _C_Q, _C_K, _C_VV, _C_QI, _C_KI, _C_WI, _C_END = 0, 512, 1024, 1536, 2048, 2560, 3072, 3200, 3328


def _premix_kernel(x_ref, sh_ref, sc_ref, g_ref, w_ref, lg_ref, lb_ref, ws_ref, bs_ref,
                   q_ref, k_ref, kb_ref, v_ref, vb_ref, qi_ref, ki_ref, kib_ref, wi_ref, oa_ref, vn_ref,
                   *, tm, chunked):
    x = x_ref[...]
    h = _rms(x, g_ref[...]) * (1.0 + sc_ref[...]) + sh_ref[...]
    p = jnp.dot(h.astype(BF16), w_ref[...], preferred_element_type=F32)
    k = p[:, _C_K:_C_VV]
    vv = p[:, _C_VV:_C_QI]
    ki = p[:, _C_KI:_C_KI + IDX_DIM]
    q_ref[...] = (p[:, _C_Q:_C_K] * (HEAD_DIM ** -0.5)).astype(BF16)
    k_ref[...] = k
    kb_ref[...] = k.astype(BF16)
    v_ref[...] = vv
    vb_ref[...] = vv.astype(BF16)
    qi_ref[...] = (p[:, _C_QI:_C_KI] * (IDX_DIM ** -0.5)).astype(BF16)
    ki_ref[...] = ki
    kib_ref[...] = ki.astype(BF16)
    wi_ref[...] = p[:, _C_WI:_C_WI + IDX_HEADS] * (IDX_HEADS ** -0.5)
    gu = _gelu_tanh(p[:, _C_U:_C_V])
    gv = _gelu_tanh(p[:, _C_V:_C_Q])
    mu = jnp.mean(gv, axis=-1, keepdims=True)
    dv = gv - mu
    var = jnp.mean(dv * dv, axis=-1, keepdims=True)
    vn = dv * lax.rsqrt(var + EPS) * lg_ref[...] + lb_ref[...]
    vn_ref[...] = vn
    if chunked:
        gw = vn.shape[1] // GMLP_GROUPS
        vnb = vn.astype(BF16)
        for r in range(tm // CHUNK):
            rs = slice(r * CHUNK, (r + 1) * CHUNK)
            for g in range(GMLP_GROUPS):
                cs = slice(g * gw, (g + 1) * gw)
                s = jnp.dot(ws_ref[g], vnb[rs, cs], preferred_element_type=F32) + bs_ref[:, cs]
                oa_ref[rs, cs] = (gu[rs, cs] * s).astype(BF16)
    else:
        oa_ref[...] = (gu * (vn * ws_ref[...] + bs_ref[...])).astype(BF16)


def _premix(x, shift, scale, g_pre, w_in_p, ln_g, ln_b, ws, bs, *, tm, chunked):
    t, d = x.shape
    per_row = shift.shape[0] != 1
    mod_spec = pl.BlockSpec((tm, d), lambda i: (i, 0)) if per_row else pl.BlockSpec((1, d), lambda i: (0, 0))
    const2 = lambda a: pl.BlockSpec(a.shape, lambda i: (0, 0))
    ws_spec = pl.BlockSpec(ws.shape, lambda i: (0, 0, 0)) if chunked else const2(ws)
    row = lambda n: pl.BlockSpec((tm, n), lambda i: (i, 0))
    outs = [((t, 512), BF16), ((t, 512), F32), ((t, 512), BF16), ((t, 512), F32), ((t, 512), BF16),
            ((t, 512), BF16), ((t, IDX_DIM), F32), ((t, IDX_DIM), BF16), ((t, IDX_HEADS), F32),
            ((t, 512), BF16), ((t, 512), F32)]
    return pl.pallas_call(
        functools.partial(_premix_kernel, tm=tm, chunked=chunked),
        out_shape=[jax.ShapeDtypeStruct(s, dt) for s, dt in outs],
        grid=(t // tm,),
        in_specs=[row(d), mod_spec, mod_spec, const2(g_pre), const2(w_in_p), const2(ln_g), const2(ln_b),
                  ws_spec, const2(bs)],
        out_specs=[row(s[1]) for s, _ in outs],
        compiler_params=_cparams("arbitrary"),
        name="premix_chunked" if chunked else "premix_rows",
    )(x, shift, scale, g_pre, w_in_p, ln_g, ln_b, ws, bs)


def _dsa_prompt_kernel(qz_ref, qit_ref, wt_ref, kidx_ref, k_ref, vt_ref, o_ref,
                       s_ref, m_ref, l_ref, acc_ref, *, tq, tk, topk, nbits):
    q0 = pl.program_id(0) * tq
    n_chunks = (q0 + tq + tk - 1) // tk
    qpos = q0 + lax.broadcasted_iota(I32, (1, tq), 1)
    w = wt_ref[...]

    def key_pos(c):
        return c * tk + lax.broadcasted_iota(I32, (tk, 1), 0)

    def p1(c, carry):
        k0 = pl.multiple_of(c * tk, tk)
        kc = kidx_ref[pl.ds(k0, tk), :]
        acc = jnp.zeros((tk, tq), F32)
        for h in range(IDX_HEADS):
            d = jnp.dot(kc, qit_ref[h * IDX_DIM:(h + 1) * IDX_DIM, :], preferred_element_type=F32)
            acc = acc + w[h:h + 1, :] * jnp.maximum(d, 0.0)
        s_ref[pl.ds(k0, tk), :] = jnp.where(key_pos(c) <= qpos, acc, NEG_INF)
        return carry

    lax.fori_loop(0, n_chunks, p1, 0)

    def count(pred):
        def body(c, cnt):
            k0 = pl.multiple_of(c * tk, tk)
            hit = jnp.where(pred(s_ref[pl.ds(k0, tk), :], c), 1.0, 0.0)
            return cnt + jnp.sum(hit.reshape(tk // 8, 8, tq), axis=0)
        c8 = lax.fori_loop(0, n_chunks, body, jnp.zeros((8, tq), F32))
        return jnp.sum(c8, axis=0, keepdims=True)

    kf = float(topk)
    cnt0 = count(lambda s, c: s >= 0.0)
    base0 = jnp.where(cnt0 >= kf, jnp.int32(0), jnp.int32(INT_MIN))

    def bisect(b, base):
        cand = base | jnp.left_shift(jnp.int32(1), 30 - b)
        thr = _ordered_bits_to_float(cand)
        return jnp.where(count(lambda s, c: s >= thr) >= kf, cand, base)

    base = lax.fori_loop(0, 31, bisect, base0)
    select_all = (qpos + 1) <= topk
    tau = jnp.where(select_all, NEG_INF, _ordered_bits_to_float(base))
    cnt_gt = count(lambda s, c: s > tau)
    cnt_ge = count(lambda s, c: s >= tau)
    straddle = jnp.logical_and(cnt_ge > kf, jnp.logical_not(select_all))
    need = kf - cnt_gt

    def tie_search():
        def step(b, j):
            cand = j | jnp.left_shift(jnp.int32(1), nbits - 1 - b)
            below = count(lambda s, c: jnp.logical_and(s == tau, key_pos(c) < cand))
            return jnp.where(below < need, cand, j)
        return lax.fori_loop(0, nbits, step, jnp.zeros((1, tq), I32))

    any_straddle = jnp.max(jnp.where(straddle, 1, 0)) > 0
    j_tie = lax.cond(any_straddle, tie_search, lambda: jnp.zeros((1, tq), I32))
    j_tie = jnp.where(straddle, j_tie, jnp.int32(INT_MAX))

    m_ref[...] = jnp.full(m_ref.shape, NEG_INF, F32)
    l_ref[...] = jnp.zeros(l_ref.shape, F32)
    acc_ref[...] = jnp.zeros(acc_ref.shape, F32)
    hpg = 256 // HEAD_DIM

    def p3(c, carry):
        k0 = pl.multiple_of(c * tk, tk)
        s = s_ref[pl.ds(k0, tk), :]
        kp = key_pos(c)
        sel = jnp.logical_or(s > tau, jnp.logical_and(s == tau, kp <= j_tie))
        sel = jnp.logical_and(sel, kp <= qpos)
        for h in range(N_HEADS):
            g = h // hpg
            kc = k_ref[pl.ds(k0, tk), g * 256:(g + 1) * 256]
            lg = jnp.dot(kc, qz_ref[h], preferred_element_type=F32)
            lg = jnp.where(sel, lg, NEG_INF)
            m_old = m_ref[h:h + 1, :]
            m_new = jnp.maximum(m_old, jnp.max(lg, axis=0, keepdims=True))
            m_safe = jnp.where(m_new == NEG_INF, 0.0, m_new)
            p = jnp.exp(lg - m_safe)
            alpha = jnp.exp(m_old - m_safe)
            l_ref[h:h + 1, :] = alpha * l_ref[h:h + 1, :] + jnp.sum(p, axis=0, keepdims=True)
            hs = slice(h * HEAD_DIM, (h + 1) * HEAD_DIM)
            pv = jnp.dot(vt_ref[hs, pl.ds(k0, tk)], p.astype(BF16), preferred_element_type=F32)
            acc_ref[hs, :] = alpha * acc_ref[hs, :] + pv
            m_ref[h:h + 1, :] = m_new
        return carry

    lax.fori_loop(0, n_chunks, p3, 0)
    for h in range(N_HEADS):
        hs = slice(h * HEAD_DIM, (h + 1) * HEAD_DIM)
        acc_ref[hs, :] = acc_ref[hs, :] * (1.0 / l_ref[h:h + 1, :])
    o_ref[...] = jnp.transpose(acc_ref[...]).astype(BF16)


def _dsa_prompt(qz, qit, wt, kidx_b, k_b, vt_b, *, tq, tk):
    t = k_b.shape[0]
    topk = min(TOPK_MAX, t // 4)
    nbits = max(1, math.ceil(math.log2(t)))
    resident = lambda a: pl.BlockSpec(a.shape, lambda i: (0,) * a.ndim, pipeline_mode=pl.Buffered(1))
    return pl.pallas_call(
        functools.partial(_dsa_prompt_kernel, tq=tq, tk=tk, topk=topk, nbits=nbits),
        out_shape=jax.ShapeDtypeStruct((t, N_HEADS * HEAD_DIM), BF16),
        grid=(t // tq,),
        in_specs=[pl.BlockSpec((N_HEADS, 256, tq), lambda i: (0, 0, i)),
                  pl.BlockSpec((IDX_HEADS * IDX_DIM, tq), lambda i: (0, i)),
                  pl.BlockSpec((IDX_HEADS, tq), lambda i: (0, i)),
                  resident(kidx_b), resident(k_b), resident(vt_b)],
        out_specs=pl.BlockSpec((tq, N_HEADS * HEAD_DIM), lambda i: (i, 0)),
        scratch_shapes=[pltpu.VMEM((t, tq), F32), pltpu.VMEM((N_HEADS, tq), F32),
                        pltpu.VMEM((N_HEADS, tq), F32), pltpu.VMEM((N_HEADS * HEAD_DIM, tq), F32)],
        compiler_params=_cparams("arbitrary"),
        name="dsa_prompt",
    )(qz, qit, wt, kidx_b, k_b, vt_b)


def _idx_sample_kernel(pt_ref, q_ref, w_ref, kn_ref, *rest, pg):
    page_refs, s_ref, sn_ref = rest[:pg], rest[pg], rest[pg + 1]
    q = q_ref[...]
    w = w_ref[...]
    for r in range(pg):
        kp = page_refs[r][...].astype(BF16)
        d = lax.dot_general(q, kp, (((1,), (1,)), ((), ())), preferred_element_type=F32)
        s_ref[r:r + 1, :] = jnp.sum(w * jnp.maximum(d, 0.0), axis=0, keepdims=True)
    kn = kn_ref[...].astype(BF16).astype(F32)
    dn = jnp.sum(q.astype(F32) * kn, axis=1, keepdims=True)
    sn = jnp.sum(w * jnp.maximum(dn, 0.0), axis=0, keepdims=True)
    sn_ref[...] = jnp.broadcast_to(sn, sn_ref.shape)


def _idx_sample(page_flat, qi_s, w_s, kidx_new, cache_kidx, *, n_pages, pg):
    bd = qi_s.shape[0]
    page_spec = lambda r: pl.BlockSpec((None, PAGE_SIZE, IDX_DIM),
                                       lambda b, j, pt: (pt[b * n_pages + j * pg + r], 0, 0))
    grid_spec = pltpu.PrefetchScalarGridSpec(
        num_scalar_prefetch=1,
        grid=(bd, n_pages // pg),
        in_specs=[pl.BlockSpec((None, IDX_HEADS, IDX_DIM), lambda b, j, pt: (b, 0, 0)),
                  pl.BlockSpec((None, IDX_HEADS, 1), lambda b, j, pt: (b, 0, 0)),
                  pl.BlockSpec((None, 1, IDX_DIM), lambda b, j, pt: (b, 0, 0))]
                 + [page_spec(r) for r in range(pg)],
        out_specs=[pl.BlockSpec((None, pg, PAGE_SIZE), lambda b, j, pt: (b, j, 0)),
                   pl.BlockSpec((None, 1, LANES), lambda b, j, pt: (b, 0, 0))],
    )
    return pl.pallas_call(
        functools.partial(_idx_sample_kernel, pg=pg),
        out_shape=[jax.ShapeDtypeStruct((bd, n_pages, PAGE_SIZE), F32),
                   jax.ShapeDtypeStruct((bd, 1, LANES), F32)],
        grid_spec=grid_spec,
        compiler_params=_cparams("arbitrary", "arbitrary"),
        name="idx_sample",
    )(page_flat, qi_s, w_s, kidx_new, *([cache_kidx] * pg))


def _topk_sample_kernel(s_ref, sn_ref, mask_ref, mnew_ref, *, topk, nbits):
    s = s_ref[...]
    sn = sn_ref[...][:, 0:1]
    n_past = s.shape[0] * s.shape[1]
    kpos = (lax.broadcasted_iota(I32, s.shape, 0) * s.shape[1] + lax.broadcasted_iota(I32, s.shape, 1))
    kf = float(topk)

    def count(pred_past, pred_new):
        c = jnp.sum(jnp.sum(jnp.where(pred_past, 1.0, 0.0), axis=0, keepdims=True), axis=1, keepdims=True)
        return c + jnp.where(pred_new, 1.0, 0.0)

    cnt0 = count(s >= 0.0, sn >= 0.0)
    base0 = jnp.where(cnt0 >= kf, jnp.int32(0), jnp.int32(INT_MIN))

    def bisect(b, base):
        cand = base | jnp.left_shift(jnp.int32(1), 30 - b)
        thr = _ordered_bits_to_float(cand)
        return jnp.where(count(s >= thr, sn >= thr) >= kf, cand, base)

    tau = _ordered_bits_to_float(lax.fori_loop(0, 31, bisect, base0))
    need = kf - count(s > tau, sn > tau)

    def step(b, j):
        cand = j | jnp.left_shift(jnp.int32(1), nbits - 1 - b)
        below = count(jnp.logical_and(s == tau, kpos < cand), jnp.logical_and(sn == tau, n_past < cand))
        return jnp.where(below < need, cand, j)

    j_tie = lax.fori_loop(0, nbits, step, jnp.zeros((1, 1), I32))
    sel = jnp.logical_or(s > tau, jnp.logical_and(s == tau, kpos <= j_tie))
    sel_new = jnp.logical_or(sn > tau, jnp.logical_and(sn == tau, n_past <= j_tie))
    mask_ref[...] = jnp.where(sel, 1.0, 0.0)
    mnew_ref[...] = jnp.broadcast_to(jnp.where(sel_new, 1.0, 0.0), mnew_ref.shape)


def _topk_sample(scores, snew):
    bd, n_pages, _ = scores.shape
    length = n_pages * PAGE_SIZE + 1
    topk = min(TOPK_MAX, length // 4)
    nbits = max(1, math.ceil(math.log2(length)))
    return pl.pallas_call(
        functools.partial(_topk_sample_kernel, topk=topk, nbits=nbits),
        out_shape=[jax.ShapeDtypeStruct(scores.shape, F32), jax.ShapeDtypeStruct(snew.shape, F32)],
        grid=(bd,),
        in_specs=[pl.BlockSpec((None, n_pages, PAGE_SIZE), lambda b: (b, 0, 0)),
                  pl.BlockSpec((None, 1, LANES), lambda b: (b, 0, 0))],
        out_specs=[pl.BlockSpec((None, n_pages, PAGE_SIZE), lambda b: (b, 0, 0)),
                   pl.BlockSpec((None, 1, LANES), lambda b: (b, 0, 0))],
        compiler_params=_cparams("arbitrary"),
        name="topk_sample",
    )(scores, snew)


def _attn_sample_kernel(pt_ref, qz_ref, kn_ref, vn_ref, mask_ref, mnew_ref, *rest, pg):
    k_refs, v_refs = rest[:pg], rest[pg:2 * pg]
    o_ref, m_ref, l_ref, acc_ref = rest[2 * pg:]
    j = pl.program_id(1)
    qz = qz_ref[...]
    d_attn = qz.shape[1]

    @pl.when(j == 0)
    def _():
        sel_new = mnew_ref[...][:, 0:1] > 0.0
        kn = kn_ref[...].astype(BF16).astype(F32)
        lg = jnp.sum(qz.astype(F32) * kn, axis=1, keepdims=True)
        m_ref[...] = jnp.where(sel_new, lg, NEG_INF)
        l_ref[...] = jnp.where(sel_new, jnp.ones_like(lg), 0.0)
        vn = vn_ref[...].astype(BF16).astype(F32)
        acc_ref[...] = jnp.where(sel_new, jnp.broadcast_to(vn, acc_ref.shape), 0.0)

    for r in range(pg):
        kp = k_refs[r][...].astype(BF16)
        lg = lax.dot_general(qz, kp, (((1,), (1,)), ((), ())), preferred_element_type=F32)
        lg = jnp.where(mask_ref[r:r + 1, :] > 0.0, lg, NEG_INF)
        m_old = m_ref[...]
        m_new = jnp.maximum(m_old, jnp.max(lg, axis=1, keepdims=True))
        m_safe = jnp.where(m_new == NEG_INF, 0.0, m_new)
        p = jnp.exp(lg - m_safe)
        alpha = jnp.exp(m_old - m_safe)
        l_ref[...] = alpha * l_ref[...] + jnp.sum(p, axis=1, keepdims=True)
        pv = jnp.dot(p.astype(BF16), v_refs[r][...].astype(BF16), preferred_element_type=F32)
        acc_ref[...] = alpha * acc_ref[...] + pv
        m_ref[...] = m_new

    @pl.when(j == pl.num_programs(1) - 1)
    def _():
        head_of_lane = lax.broadcasted_iota(I32, (N_HEADS, d_attn), 1) // HEAD_DIM
        own = head_of_lane == lax.broadcasted_iota(I32, (N_HEADS, d_attn), 0)
        o = jnp.where(own, acc_ref[...] * (1.0 / l_ref[...]), 0.0)
        o_ref[...] = jnp.sum(o, axis=0, keepdims=True)


def _attn_sample(page_flat, qz_s, k_new, v_new, mask, mnew, cache_k, cache_v, *, n_pages, pg):
    bd, _, d_attn = qz_s.shape
    page_spec = lambda r: pl.BlockSpec((None, PAGE_SIZE, d_attn),
                                       lambda b, j, pt: (pt[b * n_pages + j * pg + r], 0, 0))
    per_b = lambda n, w: pl.BlockSpec((None, n, w), lambda b, j, pt: (b, 0, 0))
    grid_spec = pltpu.PrefetchScalarGridSpec(
        num_scalar_prefetch=1,
        grid=(bd, n_pages // pg),
        in_specs=[per_b(N_HEADS, d_attn), per_b(1, d_attn), per_b(1, d_attn),
                  pl.BlockSpec((None, pg, PAGE_SIZE), lambda b, j, pt: (b, j, 0)),
                  per_b(1, LANES)]
                 + [page_spec(r) for r in range(pg)] * 2,
        out_specs=per_b(1, d_attn),
        scratch_shapes=[pltpu.VMEM((N_HEADS, 1), F32), pltpu.VMEM((N_HEADS, 1), F32),
                        pltpu.VMEM((N_HEADS, d_attn), F32)],
    )
    return pl.pallas_call(
        functools.partial(_attn_sample_kernel, pg=pg),
        out_shape=jax.ShapeDtypeStruct((bd, 1, d_attn), F32),
        grid_spec=grid_spec,
        compiler_params=_cparams("arbitrary", "arbitrary"),
        name="attn_sample",
    )(page_flat, qz_s, k_new, v_new, mask, mnew, *([cache_k] * pg), *([cache_v] * pg))


def _postmix_kernel(oa_ref, ob_ref, woa_ref, wob_ref, x_ref, g1_ref, sh_ref, sc_ref, gpm_ref, gpf_ref,
                    wr_ref, br_ref, x1_ref, h2_ref, se_ref, sw_ref):
    mix = (jnp.dot(oa_ref[...], woa_ref[...], preferred_element_type=F32)
           + jnp.dot(ob_ref[...], wob_ref[...], preferred_element_type=F32))
    x1 = x_ref[...] + g1_ref[...] * _rms(mix, gpm_ref[...])
    x1_ref[...] = x1
    h2 = _rms(x1, gpf_ref[...]) * (1.0 + sc_ref[...]) + sh_ref[...]
    h2b = h2.astype(BF16)
    h2_ref[...] = h2b
    logits = jnp.dot(h2b, wr_ref[...], preferred_element_type=F32) + br_ref[...]
    lane = lax.broadcasted_iota(I32, logits.shape, 1)
    lane_f = lane.astype(F32)
    se = jnp.zeros(logits.shape, F32)
    sw = jnp.zeros(logits.shape, F32)
    top = None
    denom = None
    for r in range(TOP_K_EXPERTS):
        m = jnp.max(logits, axis=1, keepdims=True)
        idx = jnp.min(jnp.where(logits == m, lane_f, float(LANES)), axis=1, keepdims=True)
        if r == 0:
            top = m
        e = jnp.exp(m - top)
        denom = e if r == 0 else denom + e
        se = jnp.where(lane == r, idx, se)
        sw = jnp.where(lane == r, e, sw)
        logits = jnp.where(lane_f == idx, NEG_INF, logits)
    se_ref[...] = se.astype(I32)
    sw_ref[...] = sw * (1.0 / denom)


def _postmix(out_a, out_b, wo_a, wo_b, x, gate1, shift2, scale2, g_pm, g_pf, wr_p, br_p, *, tm):
    t, d = x.shape
    per_row = gate1.shape[0] != 1
    mod_spec = pl.BlockSpec((tm, d), lambda i: (i, 0)) if per_row else pl.BlockSpec((1, d), lambda i: (0, 0))
    const2 = lambda a: pl.BlockSpec(a.shape, lambda i: (0, 0))
    row = lambda n: pl.BlockSpec((tm, n), lambda i: (i, 0))
    return pl.pallas_call(
        _postmix_kernel,
        out_shape=[jax.ShapeDtypeStruct((t, d), F32), jax.ShapeDtypeStruct((t, d), BF16),
                   jax.ShapeDtypeStruct((t, LANES), I32), jax.ShapeDtypeStruct((t, LANES), F32)],
        grid=(t // tm,),
        in_specs=[row(out_a.shape[1]), row(out_b.shape[1]), const2(wo_a), const2(wo_b), row(d),
                  mod_spec, mod_spec, mod_spec, const2(g_pm), const2(g_pf), const2(wr_p), const2(br_p)],
        out_specs=[row(d), row(d), row(LANES), row(LANES)],
        compiler_params=_cparams("arbitrary"),
        name="postmix",
    )(out_a, out_b, wo_a, wo_b, x, gate1, shift2, scale2, g_pm, g_pf, wr_p, br_p)


def _moe_kernel(te_ref, nu_ref, x_ref, rw_ref, wgu_ref, bgu_ref, wd_ref, bd_ref, o_ref, wgu_b, wd_b, *, d_ff):
    i = pl.program_id(0)
    changed = jnp.logical_or(i == 0, te_ref[i] != te_ref[jnp.maximum(i - 1, 0)])

    @pl.when(changed)
    def _():
        rows = 128
        def cast(r, carry):
            r0 = pl.multiple_of(r * rows, rows)
            wgu_b[pl.ds(r0, rows), :] = wgu_ref[pl.ds(r0, rows), :].astype(BF16)
            wd_b[pl.ds(r0, rows), :] = wd_ref[pl.ds(r0, rows), :].astype(BF16)
            return carry
        lax.fori_loop(0, wgu_b.shape[0] // rows, cast, 0)

    @pl.when(i < nu_ref[0])
    def _():
        gu = jnp.dot(x_ref[...], wgu_b[...], preferred_element_type=F32) + bgu_ref[...]
        gate = jnp.minimum(gu[:, :d_ff], SWIGLU_LIMIT)
        up = jnp.clip(gu[:, d_ff:], -SWIGLU_LIMIT, SWIGLU_LIMIT)
        a = (up + 1.0) * (gate * (1.0 / (1.0 + jnp.exp(-SWIGLU_ALPHA * gate))))
        y = jnp.dot(a.astype(BF16), wd_b[...], preferred_element_type=F32) + bd_ref[...]
        o_ref[...] = y * rw_ref[...]

    @pl.when(i >= nu_ref[0])
    def _():
        o_ref[...] = jnp.zeros(o_ref.shape, F32)


def _moe(tile_e, n_used, xs, row_w, w_gate_up, b_gate_up, w_down, b_down, *, tmoe):
    n_rows, d = xs.shape
    n_tiles = n_rows // tmoe
    d_ff = w_down.shape[1]
    assert w_gate_up.shape[1] == d and w_down.shape[1] == w_down.shape[2] == d
    grid_spec = pltpu.PrefetchScalarGridSpec(
        num_scalar_prefetch=2,
        grid=(n_tiles,),
        in_specs=[pl.BlockSpec((tmoe, d), lambda i, te, nu: (i, 0)),
                  pl.BlockSpec((tmoe, 1), lambda i, te, nu: (i, 0)),
                  pl.BlockSpec((None, d, 2 * d_ff), lambda i, te, nu: (te[i], 0, 0)),
                  pl.BlockSpec((None, 1, 2 * d_ff), lambda i, te, nu: (te[i], 0, 0)),
                  pl.BlockSpec((None, d_ff, d), lambda i, te, nu: (te[i], 0, 0)),
                  pl.BlockSpec((None, 1, d), lambda i, te, nu: (te[i], 0, 0))],
        out_specs=pl.BlockSpec((tmoe, d), lambda i, te, nu: (i, 0)),
        scratch_shapes=[pltpu.VMEM((d, 2 * d_ff), BF16), pltpu.VMEM((d_ff, d), BF16)],
    )
    return pl.pallas_call(
        functools.partial(_moe_kernel, d_ff=d_ff),
        out_shape=jax.ShapeDtypeStruct((n_rows, d), F32),
        grid_spec=grid_spec,
        compiler_params=_cparams("arbitrary"),
        name="moe",
    )(tile_e, n_used, xs, row_w, w_gate_up, b_gate_up.reshape(N_EXPERTS, 1, -1), w_down,
      b_down.reshape(N_EXPERTS, 1, -1))


def _final_kernel(y4_ref, x1_ref, g2_ref, gpf_ref, o_ref):
    f = (y4_ref[0] + y4_ref[1]) + (y4_ref[2] + y4_ref[3])
    o_ref[...] = x1_ref[...] + g2_ref[...] * _rms(f, gpf_ref[...])


def _final(y4, x1, gate2, g_post_ffn, *, tm):
    t, d = x1.shape
    per_row = gate2.shape[0] != 1
    mod_spec = pl.BlockSpec((tm, d), lambda i: (i, 0)) if per_row else pl.BlockSpec((1, d), lambda i: (0, 0))
    return pl.pallas_call(
        _final_kernel,
        out_shape=jax.ShapeDtypeStruct((t, d), F32),
        grid=(t // tm,),
        in_specs=[pl.BlockSpec((TOP_K_EXPERTS, tm, d), lambda i: (0, i, 0)),
                  pl.BlockSpec((tm, d), lambda i: (i, 0)), mod_spec,
                  pl.BlockSpec((1, d), lambda i: (0, 0))],
        out_specs=pl.BlockSpec((tm, d), lambda i: (i, 0)),
        compiler_params=_cparams("arbitrary"),
        name="final",
    )(y4, x1, gate2, g_post_ffn)


def _route(sel_e, sel_w, tmoe):
    n_tok = sel_e.shape[0]
    n_assign = n_tok * TOP_K_EXPERTS
    flat_e = sel_e.reshape(-1)
    flat_w = sel_w.reshape(-1)
    flat_tok = jnp.repeat(jnp.arange(n_tok, dtype=I32), TOP_K_EXPERTS)
    order = jnp.argsort(flat_e)
    se = flat_e[order]
    counts = jnp.zeros((N_EXPERTS,), I32).at[flat_e].add(1)
    padded = (counts + tmoe - 1) // tmoe * tmoe
    start = jnp.cumsum(counts) - counts
    pad_end = jnp.cumsum(padded)
    pad_start = pad_end - padded
    dest_sorted = pad_start[se] + (jnp.arange(n_assign, dtype=I32) - start[se])
    n_tiles = -(-n_assign // tmoe) + N_EXPERTS
    n_rows = n_tiles * tmoe
    dest = jnp.zeros((n_assign,), I32).at[order].set(dest_sorted)
    row_tok = jnp.full((n_rows,), n_tok, I32).at[dest].set(flat_tok)
    row_w = jnp.zeros((n_rows,), F32).at[dest].set(flat_w)
    tile_start = jnp.arange(n_tiles, dtype=I32) * tmoe
    tile_e = jnp.minimum(jnp.searchsorted(pad_end, tile_start, side="right"), N_EXPERTS - 1).astype(I32)
    n_used = (pad_end[-1] // tmoe).astype(I32).reshape(1)
    last_e = tile_e[jnp.maximum(n_used[0] - 1, 0)]
    tile_e = jnp.where(tile_start < pad_end[-1], tile_e, last_e)
    return row_tok, row_w, tile_e, n_used, dest.reshape(n_tok, TOP_K_EXPERTS)


def _pad_cols(a, n):
    return jnp.concatenate([a, jnp.zeros(a.shape[:-1] + (n - a.shape[-1],), a.dtype)], axis=-1)


def _head_block_rows(qt):
    d, t = qt.shape
    rows = lax.broadcasted_iota(I32, (N_HEADS, 256, 1), 1) + 256 * (lax.broadcasted_iota(I32, (N_HEADS, 256, 1), 0) // 4)
    own = (rows // HEAD_DIM) == lax.broadcasted_iota(I32, (N_HEADS, 256, 1), 0)
    grp = qt.reshape(d // 256, 256, t)
    tiled = jnp.repeat(grp, 4, axis=0)
    return jnp.where(own, tiled, jnp.zeros((), qt.dtype))


def kernel(x_prompt, x_sample, c_prompt, c_sample, cache_k, cache_v, cache_kidx, page_table, w_ada, b_ada,
           g_pre_mix, w_in, gmlp_ln_g, gmlp_ln_b, gmlp_w_s, gmlp_b_s, w_out, g_post_mix, g_pre_ffn, w_router,
           b_router, w_gate_up, b_gate_up, w_down, b_down, g_post_ffn):
    depth = w_ada.shape[0]
    assert depth == 1 and x_prompt.shape[0] == 1 and x_sample.shape[1] == 1
    _, t, d = x_prompt.shape
    bd = x_sample.shape[0]
    n_pages = page_table.shape[1]
    d_attn = N_HEADS * HEAD_DIM
    l = 0
    row2 = lambda a: a.reshape(1, -1)

    c_all = jnp.concatenate([c_prompt, c_sample], axis=0)
    m_pad = -(-c_all.shape[0] // 8) * 8
    c_all = jnp.concatenate([c_all, jnp.zeros((m_pad - c_all.shape[0], d), F32)], axis=0)
    mod = _adaln(c_all, w_ada[l], b_ada[l])
    mod_p = [mod[0:1, i * d:(i + 1) * d] for i in range(6)]
    mod_s = [mod[1:1 + bd, i * d:(i + 1) * d] for i in range(6)]

    w_in_l = w_in[l]
    w_in_p = jnp.concatenate([w_in_l[:, :_C_KI], _pad_cols(w_in_l[:, 3072:3136], LANES),
                              _pad_cols(w_in_l[:, 3136:3144], LANES)], axis=1).astype(BF16)
    tril = jnp.tril(jnp.ones((CHUNK, CHUNK), dtype=bool))
    ws_chunk = jnp.where(tril[None], gmlp_w_s[l], 0.0).astype(BF16)
    gw = 512 // GMLP_GROUPS
    bs_chunk = jnp.repeat(jnp.transpose(gmlp_b_s[l]), gw, axis=1)
    ws_row = jnp.repeat(gmlp_w_s[l][:, 0, 0], gw).reshape(1, -1)
    bs_row = jnp.repeat(gmlp_b_s[l][:, 0], gw).reshape(1, -1)
    wo = w_out[l].astype(BF16)
    wo_a, wo_b = wo[:512], wo[512:]
    wr_p = _pad_cols(w_router[l], LANES).astype(BF16)
    br_p = jnp.concatenate([b_router[l], jnp.full((LANES - N_EXPERTS,), NEG_INF, F32)]).reshape(1, LANES)
    g_pre, g_pm, g_pf, g_po = row2(g_pre_mix[l]), row2(g_post_mix[l]), row2(g_pre_ffn[l]), row2(g_post_ffn[l])
    ln_g, ln_b = row2(gmlp_ln_g[l]), row2(gmlp_ln_b[l])

    xp = x_prompt[0]
    (q_p, k_p, kb_p, v_p, vb_p, qi_p, ki_p, kib_p, wi_p, oa_p, _) = _premix(
        xp, mod_p[0], mod_p[1], g_pre, w_in_p, ln_g, ln_b, ws_chunk, bs_chunk, tm=256, chunked=True)
    qz_p = _head_block_rows(jnp.transpose(q_p))
    ob_p = _dsa_prompt(qz_p, jnp.transpose(qi_p), jnp.transpose(wi_p), kib_p, kb_p, jnp.transpose(vb_p),
                       tq=128, tk=min(512, t))
    x1_p, h2_p, se_p, sw_p = _postmix(oa_p, ob_p, wo_a, wo_b, xp, mod_p[2], mod_p[3], mod_p[4], g_pm, g_pf,
                                      wr_p, br_p, tm=256)

    xs_ = x_sample[:, 0]
    (q_s, k_s, _, v_s, _, qi_s, ki_s, _, wi_s, oa_s, vn_s) = _premix(
        xs_, mod_s[0], mod_s[1], g_pre, w_in_p, ln_g, ln_b, ws_row, bs_row, tm=bd, chunked=False)
    page_flat = page_table.reshape(-1)
    pg = 16 if n_pages % 16 == 0 else 1
    scores, snew = _idx_sample(page_flat, qi_s.reshape(bd, IDX_HEADS, IDX_DIM), wi_s.reshape(bd, IDX_HEADS, 1),
                               ki_s.reshape(bd, 1, IDX_DIM), cache_kidx[l], n_pages=n_pages, pg=pg)
    mask, mnew = _topk_sample(scores, snew)
    head_of_lane = jnp.arange(d_attn, dtype=I32) // HEAD_DIM
    qz_s = jnp.where(head_of_lane[None, None, :] == jnp.arange(N_HEADS, dtype=I32)[None, :, None],
                     q_s[:, None, :], jnp.zeros((), BF16))
    pga = 8 if n_pages % 8 == 0 else 1
    ob_s = _attn_sample(page_flat, qz_s, k_s.reshape(bd, 1, d_attn), v_s.reshape(bd, 1, d_attn), mask, mnew,
                        cache_k[l].reshape(-1, PAGE_SIZE, d_attn), cache_v[l].reshape(-1, PAGE_SIZE, d_attn),
                        n_pages=n_pages, pg=pga)
    x1_s, h2_s, se_s, sw_s = _postmix(oa_s, ob_s.reshape(bd, d_attn).astype(BF16), wo_a, wo_b, xs_, mod_s[2],
                                      mod_s[3], mod_s[4], g_pm, g_pf, wr_p, br_p, tm=bd)

    tmoe = 256
    h2_all = jnp.concatenate([h2_p, h2_s, jnp.zeros((1, d), BF16)], axis=0)
    sel_e = jnp.concatenate([se_p[:, :TOP_K_EXPERTS], se_s[:, :TOP_K_EXPERTS]], axis=0)
    sel_w = jnp.concatenate([sw_p[:, :TOP_K_EXPERTS], sw_s[:, :TOP_K_EXPERTS]], axis=0)
    row_tok, row_w, tile_e, n_used, dest = _route(sel_e, sel_w, tmoe)
    ys = _moe(tile_e, n_used, h2_all[row_tok], row_w.reshape(-1, 1), w_gate_up[l], b_gate_up[l], w_down[l],
              b_down[l], tmoe=tmoe)
    y_p = _final(ys[jnp.transpose(dest[:t])], x1_p, mod_p[5], g_po, tm=256)
    y_s = _final(ys[jnp.transpose(dest[t:])], x1_s, mod_s[5], g_po, tm=bd)

    hs = (N_HEADS, HEAD_DIM)
    return (y_p[None], y_s[:, None],
            k_p.reshape(1, 1, t, *hs), v_p.reshape(1, 1, t, *hs), ki_p.reshape(1, 1, t, IDX_DIM),
            k_s.reshape(1, bd, 1, *hs), v_s.reshape(1, bd, 1, *hs), ki_s.reshape(1, bd, 1, IDX_DIM),
            vn_s.reshape(1, bd, 1, -1))
```

```python
import functools
import math

import jax
import jax.numpy as jnp
from jax import lax
from jax.experimental import pallas as pl
from jax.experimental.pallas import tpu as pltpu

F32 = jnp.float32
BF16 = jnp.bfloat16
I32 = jnp.int32

EPS = 1e-6
N_HEADS = 8
HEAD_DIM = 64
IDX_HEADS = 8
IDX_DIM = 64
GMLP_GROUPS = 4
CHUNK = 128
TOPK_MAX = 256
PAGE_SIZE = 128
N_EXPERTS = 32
TOP_K_EXPERTS = 4
SWIGLU_LIMIT = 7.0
SWIGLU_ALPHA = 1.702
LOG2E = 1.4426950408889634

LANES = 128
SUBLANES = 8
MXU_DEPTH = 256
VMEM_LIMIT = 60000 * 1024
INT_MIN = -2147483648
INT_MAX = 2147483647
NEG_INF = float("-inf")


def _cparams(*sem):
    return pltpu.CompilerParams(dimension_semantics=sem, vmem_limit_bytes=VMEM_LIMIT)


def _gelu_tanh(x):
    return 0.5 * x * (1.0 + jnp.tanh(0.7978845608028654 * (x + 0.044715 * (x * x * x))))


def _rms(x, g):
    return x * lax.rsqrt(jnp.mean(x * x, axis=-1, keepdims=True) + EPS) * g


def _ordered_bits_to_float(u):
    bits = jnp.where(u >= 0, u, u ^ jnp.int32(0x7FFFFFFF))
    return lax.bitcast_convert_type(bits, F32)


def _col_reduce(x, op, rows):
    n, w = x.shape
    part = op(x.reshape(n // rows, rows, w), axis=0)
    return op(part, axis=0, keepdims=True)


def _adaln_kernel(c_ref, w_ref, b_ref, o_ref):
    c = c_ref[...]
    a = c * (1.0 / (1.0 + jnp.exp(-c)))
    o_ref[...] = jnp.dot(a.astype(BF16), w_ref[...].astype(BF16), preferred_element_type=F32) + b_ref[...]


def _adaln(c_all, w_ada, b_ada):
    m, d = c_all.shape
    n = w_ada.shape[1]
    tn = 1536
    return pl.pallas_call(
        _adaln_kernel,
        out_shape=jax.ShapeDtypeStruct((m, n), F32),
        grid=(n // tn,),
        in_specs=[pl.BlockSpec((m, d), lambda j: (0, 0)),
                  pl.BlockSpec((d, tn), lambda j: (0, j)),
                  pl.BlockSpec((1, tn), lambda j: (0, j))],
        out_specs=pl.BlockSpec((m, tn), lambda j: (0, j)),
        compiler_params=_cparams("arbitrary"),
        name="adaln",
    )(c_all, w_ada, b_ada.reshape(1, n))


_C_U, _C_V, _C_Q, _C_K, _C_VV, _C_QI, _C_KI, _C_WI, _C_END = 0, 512, 1024, 1536, 2048, 2560, 3072, 3200, 3328


def _premix_kernel(x_ref, sh_ref, sc_ref, g_ref, w_ref, lg_ref, lb_ref, ws_ref, bs_ref,
                   q_ref, k_ref, kb_ref, v_ref, vb_ref, qi_ref, ki_ref, kib_ref, wi_ref, oa_ref, vn_ref,
                   *, tm, chunked):
    x = x_ref[...]
    h = _rms(x, g_ref[...]) * (1.0 + sc_ref[...]) + sh_ref[...]
    p = jnp.dot(h.astype(BF16), w_ref[...], preferred_element_type=F32)
    k = p[:, _C_K:_C_VV]
    vv = p[:, _C_VV:_C_QI]
    ki = p[:, _C_KI:_C_KI + IDX_DIM]
    q_ref[...] = (p[:, _C_Q:_C_K] * (HEAD_DIM ** -0.5 * LOG2E)).astype(BF16)
    k_ref[...] = k
    kb_ref[...] = k.astype(BF16)
    v_ref[...] = vv
    vb_ref[...] = vv.astype(BF16)
    qi_ref[...] = (p[:, _C_QI:_C_KI] * (IDX_DIM ** -0.5)).astype(BF16)
    ki_ref[...] = ki
    kib_ref[...] = ki.astype(BF16)
    wi_ref[...] = p[:, _C_WI:_C_WI + IDX_HEADS] * (IDX_HEADS ** -0.5)
    gu = _gelu_tanh(p[:, _C_U:_C_V])
    gv = _gelu_tanh(p[:, _C_V:_C_Q])
    mu = jnp.mean(gv, axis=-1, keepdims=True)
    dv = gv - mu
    var = jnp.mean(dv * dv, axis=-1, keepdims=True)
    vn = dv * lax.rsqrt(var + EPS) * lg_ref[...] + lb_ref[...]
    vn_ref[...] = vn
    if chunked:
        gw = vn.shape[1] // GMLP_GROUPS
        vnb = vn.astype(BF16)
        for r in range(tm // CHUNK):
            rs = slice(r * CHUNK, (r + 1) * CHUNK)
            for g in range(GMLP_GROUPS):
                cs = slice(g * gw, (g + 1) * gw)
                s = jnp.dot(ws_ref[g], vnb[rs, cs], preferred_element_type=F32) + bs_ref[:, cs]
                oa_ref[rs, cs] = (gu[rs, cs] * s).astype(BF16)
    else:
        oa_ref[...] = (gu * (vn * ws_ref[...] + bs_ref[...])).astype(BF16)


def _premix(x, shift, scale, g_pre, w_in_p, ln_g, ln_b, ws, bs, *, tm, chunked):
    t, d = x.shape
    per_row = shift.shape[0] != 1
    mod_spec = pl.BlockSpec((tm, d), lambda i: (i, 0)) if per_row else pl.BlockSpec((1, d), lambda i: (0, 0))
    const2 = lambda a: pl.BlockSpec(a.shape, lambda i: (0, 0))
    ws_spec = pl.BlockSpec(ws.shape, lambda i: (0, 0, 0)) if chunked else const2(ws)
    row = lambda n: pl.BlockSpec((tm, n), lambda i: (i, 0))
    outs = [((t, 512), BF16), ((t, 512), F32), ((t, 512), BF16), ((t, 512), F32), ((t, 512), BF16),
            ((t, 512), BF16), ((t, IDX_DIM), F32), ((t, IDX_DIM), BF16), ((t, IDX_HEADS), F32),
            ((t, 512), BF16), ((t, 512), F32)]
    return pl.pallas_call(
        functools.partial(_premix_kernel, tm=tm, chunked=chunked),
        out_shape=[jax.ShapeDtypeStruct(s, dt) for s, dt in outs],
        grid=(t // tm,),
        in_specs=[row(d), mod_spec, mod_spec, const2(g_pre), const2(w_in_p), const2(ln_g), const2(ln_b),
                  ws_spec, const2(bs)],
        out_specs=[row(s[1]) for s, _ in outs],
        compiler_params=_cparams("arbitrary"),
        name="premix_chunked" if chunked else "premix_rows",
    )(x, shift, scale, g_pre, w_in_p, ln_g, ln_b, ws, bs)


_CNT_ROWS = 64
_RED_ROWS = 32


def _dsa_prompt_kernel(qz2_ref, qit_ref, wt_ref, kidx_ref, k_ref, vt_ref, o_ref,
                       s_ref, m_ref, l_ref, acc_ref, lg_ref, p_ref, *, tq, tk, tk3, topk, nbits):
    q0 = pl.program_id(0) * tq
    n_chunks = (q0 + tq + tk - 1) // tk
    n_chunks3 = (q0 + tq + tk3 - 1) // tk3
    qpos = q0 + lax.broadcasted_iota(I32, (1, tq), 1)
    w = wt_ref[...]

    def key_pos(c, size):
        return c * size + lax.broadcasted_iota(I32, (size, 1), 0)

    def p1(c, carry):
        k0 = pl.multiple_of(c * tk, tk)
        kc = kidx_ref[pl.ds(k0, tk), :]
        acc = jnp.zeros((tk, tq), F32)
        for h in range(IDX_HEADS):
            d = jnp.dot(kc, qit_ref[h * IDX_DIM:(h + 1) * IDX_DIM, :], preferred_element_type=F32)
            acc = acc + w[h:h + 1, :] * jnp.maximum(d, 0.0)
        s_ref[pl.ds(k0, tk), :] = jnp.where(key_pos(c, tk) <= qpos, acc, NEG_INF)
        return carry

    lax.fori_loop(0, n_chunks, p1, 0)

    def count(pred):
        def body(c, acc):
            k0 = pl.multiple_of(c * tk, tk)
            hit = jnp.where(pred(s_ref[pl.ds(k0, tk), :], c), 1.0, 0.0)
            return acc + jnp.sum(hit.reshape(tk // _CNT_ROWS, _CNT_ROWS, tq), axis=0)
        acc = lax.fori_loop(0, n_chunks, body, jnp.zeros((_CNT_ROWS, tq), F32))
        return jnp.sum(acc, axis=0, keepdims=True)

    kf = float(topk)
    select_all = (qpos + 1) <= topk
    cnt0 = count(lambda s, c: s >= 0.0)
    nonneg = cnt0 >= kf
    base0 = jnp.where(nonneg, jnp.int32(0), jnp.int32(INT_MIN))
    cntb0 = jnp.where(nonneg, cnt0, (qpos + 1).astype(F32))

    def bisect(b, state):
        base, cntb = state
        cand = base | jnp.left_shift(jnp.int32(1), 30 - b)
        thr = _ordered_bits_to_float(cand)
        c = count(lambda s, cc: s >= thr)
        ok = c >= kf
        return jnp.where(ok, cand, base), jnp.where(ok, c, cntb)

    base, cntb = lax.fori_loop(0, 31, bisect, (base0, cntb0))
    tau = jnp.where(select_all, NEG_INF, _ordered_bits_to_float(base))
    straddle = jnp.logical_and(cntb > kf, jnp.logical_not(select_all))

    def tie_search():
        need = kf - count(lambda s, c: s > tau)

        def step(b, j):
            cand = j | jnp.left_shift(jnp.int32(1), nbits - 1 - b)
            below = count(lambda s, c: jnp.logical_and(s == tau, key_pos(c, tk) < cand))
            return jnp.where(below < need, cand, j)
        return lax.fori_loop(0, nbits, step, jnp.zeros((1, tq), I32))

    any_straddle = jnp.max(jnp.where(straddle, 1, 0)) > 0
    j_tie = lax.cond(any_straddle, tie_search, lambda: jnp.zeros((1, tq), I32))
    j_tie = jnp.where(straddle, j_tie, jnp.int32(INT_MAX))

    m_ref[...] = jnp.full(m_ref.shape, NEG_INF, F32)
    l_ref[...] = jnp.zeros(l_ref.shape, F32)
    acc_ref[...] = jnp.zeros(acc_ref.shape, F32)
    hpg = MXU_DEPTH // HEAD_DIM

    def p3(c, carry):
        k0 = pl.multiple_of(c * tk3, tk3)
        s = s_ref[pl.ds(k0, tk3), :]
        kp = key_pos(c, tk3)
        sel = jnp.logical_or(s > tau, jnp.logical_and(s == tau, kp <= j_tie))
        sel = jnp.logical_and(sel, kp <= qpos)
        cmax = []
        for pr in range(N_HEADS // 2):
            g = (2 * pr) // hpg
            kc = k_ref[pl.ds(k0, tk3), g * MXU_DEPTH:(g + 1) * MXU_DEPTH]
            lg2 = jnp.dot(kc, qz2_ref[pr], preferred_element_type=F32)
            for half in range(2):
                lg = jnp.where(sel, lg2[:, half * tq:(half + 1) * tq], NEG_INF)
                lg_ref[2 * pr + half] = lg
                cmax.append(_col_reduce(lg, jnp.max, _RED_ROWS))
        m_old = m_ref[...]
        m_new = jnp.maximum(m_old, jnp.concatenate(cmax, axis=0))
        m_safe = jnp.where(m_new == NEG_INF, 0.0, m_new)
        alpha = jnp.exp2(m_old - m_safe)
        m_ref[...] = m_new
        psum = []
        for h in range(N_HEADS):
            p = jnp.exp2(lg_ref[h] - m_safe[h:h + 1, :])
            psum.append(_col_reduce(p, jnp.sum, _RED_ROWS))
            p_ref[h] = p.astype(BF16)
        l_ref[...] = alpha * l_ref[...] + jnp.concatenate(psum, axis=0)
        for h in range(N_HEADS):
            hs = slice(h * HEAD_DIM, (h + 1) * HEAD_DIM)
            pv = jnp.dot(vt_ref[hs, pl.ds(k0, tk3)], p_ref[h], preferred_element_type=F32)
            acc_ref[hs, :] = alpha[h:h + 1, :] * acc_ref[hs, :] + pv
        return carry

    lax.fori_loop(0, n_chunks3, p3, 0)
    for h in range(N_HEADS):
        hs = slice(h * HEAD_DIM, (h + 1) * HEAD_DIM)
        acc_ref[hs, :] = acc_ref[hs, :] * (1.0 / l_ref[h:h + 1, :])
    o_ref[...] = jnp.transpose(acc_ref[...]).astype(BF16)


def _dsa_prompt(qz2, qit, wt, kidx_b, k_b, vt_b, *, tq, tk, tk3):
    t = k_b.shape[0]
    topk = min(TOPK_MAX, t // 4)
    nbits = max(1, math.ceil(math.log2(t)))
    resident = lambda a: pl.BlockSpec(a.shape, lambda i: (0,) * a.ndim, pipeline_mode=pl.Buffered(1))
    return pl.pallas_call(
        functools.partial(_dsa_prompt_kernel, tq=tq, tk=tk, tk3=tk3, topk=topk, nbits=nbits),
        out_shape=jax.ShapeDtypeStruct((t, N_HEADS * HEAD_DIM), BF16),
        grid=(t // tq,),
        in_specs=[pl.BlockSpec((N_HEADS // 2, MXU_DEPTH, 2 * tq), lambda i: (0, 0, i)),
                  pl.BlockSpec((IDX_HEADS * IDX_DIM, tq), lambda i: (0, i)),
                  pl.BlockSpec((IDX_HEADS, tq), lambda i: (0, i)),
                  resident(kidx_b), resident(k_b), resident(vt_b)],
        out_specs=pl.BlockSpec((tq, N_HEADS * HEAD_DIM), lambda i: (i, 0)),
        scratch_shapes=[pltpu.VMEM((t, tq), F32), pltpu.VMEM((N_HEADS, tq), F32),
                        pltpu.VMEM((N_HEADS, tq), F32), pltpu.VMEM((N_HEADS * HEAD_DIM, tq), F32),
                        pltpu.VMEM((N_HEADS, tk3, tq), F32), pltpu.VMEM((N_HEADS, tk3, tq), BF16)],
        compiler_params=_cparams("arbitrary"),
        name="dsa_prompt",
    )(qz2, qit, wt, kidx_b, k_b, vt_b)


def _idx_sample_kernel(pt_ref, q_ref, w_ref, kn_ref, *rest, pg):
    page_refs, s_ref, sn_ref = rest[:pg], rest[pg], rest[pg + 1]
    q = q_ref[...]
    w = w_ref[...]
    for r in range(pg):
        kp = page_refs[r][...].astype(BF16)
        d = lax.dot_general(q, kp, (((1,), (1,)), ((), ())), preferred_element_type=F32)
        s_ref[r:r + 1, :] = jnp.sum(w * jnp.maximum(d, 0.0), axis=0, keepdims=True)
    kn = kn_ref[...].astype(BF16).astype(F32)
    dn = jnp.sum(q.astype(F32) * kn, axis=1, keepdims=True)
    sn = jnp.sum(w * jnp.maximum(dn, 0.0), axis=0, keepdims=True)
    sn_ref[...] = jnp.broadcast_to(sn, sn_ref.shape)


def _idx_sample(page_flat, qi_s, w_s, kidx_new, cache_kidx, *, n_pages, pg):
    bd = qi_s.shape[0]
    page_spec = lambda r: pl.BlockSpec((None, PAGE_SIZE, IDX_DIM),
                                       lambda b, j, pt: (pt[b * n_pages + j * pg + r], 0, 0))
    grid_spec = pltpu.PrefetchScalarGridSpec(
        num_scalar_prefetch=1,
        grid=(bd, n_pages // pg),
        in_specs=[pl.BlockSpec((None, IDX_HEADS, IDX_DIM), lambda b, j, pt: (b, 0, 0)),
                  pl.BlockSpec((None, IDX_HEADS, 1), lambda b, j, pt: (b, 0, 0)),
                  pl.BlockSpec((None, 1, IDX_DIM), lambda b, j, pt: (b, 0, 0))]
                 + [page_spec(r) for r in range(pg)],
        out_specs=[pl.BlockSpec((None, pg, PAGE_SIZE), lambda b, j, pt: (b, j, 0)),
                   pl.BlockSpec((None, 1, LANES), lambda b, j, pt: (b, 0, 0))],
    )
    return pl.pallas_call(
        functools.partial(_idx_sample_kernel, pg=pg),
        out_shape=[jax.ShapeDtypeStruct((bd, n_pages, PAGE_SIZE), F32),
                   jax.ShapeDtypeStruct((bd, 1, LANES), F32)],
        grid_spec=grid_spec,
        compiler_params=_cparams("arbitrary", "arbitrary"),
        name="idx_sample",
    )(page_flat, qi_s, w_s, kidx_new, *([cache_kidx] * pg))


def _topk_sample_kernel(s_ref, sn_ref, mask_ref, mnew_ref, *, topk, nbits):
    s = s_ref[...]
    sn = sn_ref[...][:, 0:1]
    n_past = s.shape[0] * s.shape[1]
    kpos = (lax.broadcasted_iota(I32, s.shape, 0) * s.shape[1] + lax.broadcasted_iota(I32, s.shape, 1))
    kf = float(topk)

    def count(pred_past, pred_new):
        c = jnp.sum(jnp.sum(jnp.where(pred_past, 1.0, 0.0), axis=0, keepdims=True), axis=1, keepdims=True)
        return c + jnp.where(pred_new, 1.0, 0.0)

    cnt0 = count(s >= 0.0, sn >= 0.0)
    base0 = jnp.where(cnt0 >= kf, jnp.int32(0), jnp.int32(INT_MIN))

    def bisect(b, base):
        cand = base | jnp.left_shift(jnp.int32(1), 30 - b)
        thr = _ordered_bits_to_float(cand)
        return jnp.where(count(s >= thr, sn >= thr) >= kf, cand, base)

    tau = _ordered_bits_to_float(lax.fori_loop(0, 31, bisect, base0))
    need = kf - count(s > tau, sn > tau)

    def step(b, j):
        cand = j | jnp.left_shift(jnp.int32(1), nbits - 1 - b)
        below = count(jnp.logical_and(s == tau, kpos < cand), jnp.logical_and(sn == tau, n_past < cand))
        return jnp.where(below < need, cand, j)

    j_tie = lax.fori_loop(0, nbits, step, jnp.zeros((1, 1), I32))
    sel = jnp.logical_or(s > tau, jnp.logical_and(s == tau, kpos <= j_tie))
    sel_new = jnp.logical_or(sn > tau, jnp.logical_and(sn == tau, n_past <= j_tie))
    mask_ref[...] = jnp.where(sel, 1.0, 0.0)
    mnew_ref[...] = jnp.broadcast_to(jnp.where(sel_new, 1.0, 0.0), mnew_ref.shape)


def _topk_sample(scores, snew, topk):
    bd, n_pages, _ = scores.shape
    nbits = max(1, math.ceil(math.log2(n_pages * PAGE_SIZE + 1)))
    return pl.pallas_call(
        functools.partial(_topk_sample_kernel, topk=topk, nbits=nbits),
        out_shape=[jax.ShapeDtypeStruct(scores.shape, F32), jax.ShapeDtypeStruct(snew.shape, F32)],
        grid=(bd,),
        in_specs=[pl.BlockSpec((None, n_pages, PAGE_SIZE), lambda b: (b, 0, 0)),
                  pl.BlockSpec((None, 1, LANES), lambda b: (b, 0, 0))],
        out_specs=[pl.BlockSpec((None, n_pages, PAGE_SIZE), lambda b: (b, 0, 0)),
                   pl.BlockSpec((None, 1, LANES), lambda b: (b, 0, 0))],
        compiler_params=_cparams("arbitrary"),
        name="topk_sample",
    )(scores, snew)


def _attn_gather_kernel(slab_ref, newpos_ref, q_ref, kn_ref, vn_ref, ck_ref, cv_ref, o_ref,
                        kbuf, vbuf, sem, *, topk):
    b = pl.program_id(0)
    slot = b % 2

    def row_copy(src, buf, which, bb, sl, i):
        r0 = pl.multiple_of(slab_ref[bb * topk + i] * N_HEADS, N_HEADS)
        d0 = pl.multiple_of(i * N_HEADS, N_HEADS)
        return pltpu.make_async_copy(src.at[pl.ds(r0, N_HEADS), :], buf.at[sl, pl.ds(d0, N_HEADS), :],
                                     sem.at[which, sl])

    def for_rows(bb, sl, fn):
        def body(i, carry):
            fn(row_copy(ck_ref, kbuf, 0, bb, sl, i))
            fn(row_copy(cv_ref, vbuf, 1, bb, sl, i))
            return carry
        lax.fori_loop(0, topk, body, 0)

    @pl.when(b == 0)
    def _():
        for_rows(0, 0, lambda cp: cp.start())

    @pl.when(b + 1 < pl.num_programs(0))
    def _():
        for_rows(b + 1, 1 - slot, lambda cp: cp.start())

    for_rows(b, slot, lambda cp: cp.wait())

    @pl.when(newpos_ref[b] >= 0)
    def _():
        d0 = pl.multiple_of(newpos_ref[b] * N_HEADS, N_HEADS)
        kbuf[slot, pl.ds(d0, N_HEADS), :] = kn_ref[...]
        vbuf[slot, pl.ds(d0, N_HEADS), :] = vn_ref[...]

    kb = kbuf[slot].reshape(topk, N_HEADS, HEAD_DIM)
    vb = vbuf[slot].reshape(topk, N_HEADS, HEAD_DIM)
    q = q_ref[...].astype(F32)
    lg = jnp.sum(kb * q[None], axis=2, keepdims=True)
    m = jnp.max(lg, axis=0, keepdims=True)
    p = jnp.exp2(lg - m)
    l = jnp.sum(p, axis=0, keepdims=True)
    o_ref[...] = jnp.sum(p * vb, axis=0) * (1.0 / l[0])


def _attn_gather(slab, newpos, q_s, k_new, v_new, cache_k2, cache_v2, *, topk):
    bd = q_s.shape[0]
    per_b = pl.BlockSpec((None, N_HEADS, HEAD_DIM), lambda b, s, n: (b, 0, 0))
    grid_spec = pltpu.PrefetchScalarGridSpec(
        num_scalar_prefetch=2,
        grid=(bd,),
        in_specs=[per_b, per_b, per_b, pl.BlockSpec(memory_space=pl.ANY), pl.BlockSpec(memory_space=pl.ANY)],
        out_specs=per_b,
        scratch_shapes=[pltpu.VMEM((2, topk * N_HEADS, HEAD_DIM), F32),
                        pltpu.VMEM((2, topk * N_HEADS, HEAD_DIM), F32),
                        pltpu.SemaphoreType.DMA((2, 2))],
    )
    return pl.pallas_call(
        functools.partial(_attn_gather_kernel, topk=topk),
        out_shape=jax.ShapeDtypeStruct((bd, N_HEADS, HEAD_DIM), F32),
        grid_spec=grid_spec,
        compiler_params=_cparams("arbitrary"),
        name="attn_gather",
    )(slab, newpos, q_s, k_new, v_new, cache_k2, cache_v2)


def _postmix_kernel(oa_ref, ob_ref, woa_ref, wob_ref, x_ref, g1_ref, sh_ref, sc_ref, gpm_ref, gpf_ref,
                    wr_ref, br_ref, x1_ref, h2_ref, se_ref, sw_ref):
    mix = (jnp.dot(oa_ref[...], woa_ref[...], preferred_element_type=F32)
           + jnp.dot(ob_ref[...], wob_ref[...], preferred_element_type=F32))
    x1 = x_ref[...] + g1_ref[...] * _rms(mix, gpm_ref[...])
    x1_ref[...] = x1
    h2 = _rms(x1, gpf_ref[...]) * (1.0 + sc_ref[...]) + sh_ref[...]
    h2b = h2.astype(BF16)
    h2_ref[...] = h2b
    logits = jnp.dot(h2b, wr_ref[...], preferred_element_type=F32) + br_ref[...]
    lane = lax.broadcasted_iota(I32, logits.shape, 1)
    lane_f = lane.astype(F32)
    se = jnp.zeros(logits.shape, F32)
    sw = jnp.zeros(logits.shape, F32)
    top = None
    denom = None
    for r in range(TOP_K_EXPERTS):
        m = jnp.max(logits, axis=1, keepdims=True)
        idx = jnp.min(jnp.where(logits == m, lane_f, float(LANES)), axis=1, keepdims=True)
        if r == 0:
            top = m
        e = jnp.exp(m - top)
        denom = e if r == 0 else denom + e
        se = jnp.where(lane == r, idx, se)
        sw = jnp.where(lane == r, e, sw)
        logits = jnp.where(lane_f == idx, NEG_INF, logits)
    se_ref[...] = se.astype(I32)
    sw_ref[...] = sw * (1.0 / denom)


def _postmix(out_a, out_b, wo_a, wo_b, x, gate1, shift2, scale2, g_pm, g_pf, wr_p, br_p, *, tm):
    t, d = x.shape
    per_row = gate1.shape[0] != 1
    mod_spec = pl.BlockSpec((tm, d), lambda i: (i, 0)) if per_row else pl.BlockSpec((1, d), lambda i: (0, 0))
    const2 = lambda a: pl.BlockSpec(a.shape, lambda i: (0, 0))
    row = lambda n: pl.BlockSpec((tm, n), lambda i: (i, 0))
    return pl.pallas_call(
        _postmix_kernel,
        out_shape=[jax.ShapeDtypeStruct((t, d), F32), jax.ShapeDtypeStruct((t, d), BF16),
                   jax.ShapeDtypeStruct((t, LANES), I32), jax.ShapeDtypeStruct((t, LANES), F32)],
        grid=(t // tm,),
        in_specs=[row(out_a.shape[1]), row(out_b.shape[1]), const2(wo_a), const2(wo_b), row(d),
                  mod_spec, mod_spec, mod_spec, const2(g_pm), const2(g_pf), const2(wr_p), const2(br_p)],
        out_specs=[row(d), row(d), row(LANES), row(LANES)],
        compiler_params=_cparams("arbitrary"),
        name="postmix",
    )(out_a, out_b, wo_a, wo_b, x, gate1, shift2, scale2, g_pm, g_pf, wr_p, br_p)


def _moe_kernel(te_ref, nu_ref, x_ref, wgu_ref, bgu_ref, wd_ref, bd_ref, o_ref, wgu_b, wd_b, *, d_ff):
    i = pl.program_id(0)
    changed = jnp.logical_or(i == 0, te_ref[i] != te_ref[jnp.maximum(i - 1, 0)])

    @pl.when(changed)
    def _():
        rows = 128
        def cast(r, carry):
            r0 = pl.multiple_of(r * rows, rows)
            wgu_b[pl.ds(r0, rows), :] = wgu_ref[pl.ds(r0, rows), :].astype(BF16)
            wd_b[pl.ds(r0, rows), :] = wd_ref[pl.ds(r0, rows), :].astype(BF16)
            return carry
        lax.fori_loop(0, wgu_b.shape[0] // rows, cast, 0)

    @pl.when(i < nu_ref[0])
    def _():
        gu = jnp.dot(x_ref[...], wgu_b[...], preferred_element_type=F32) + bgu_ref[...]
        gate = jnp.minimum(gu[:, :d_ff], SWIGLU_LIMIT)
        up = jnp.clip(gu[:, d_ff:], -SWIGLU_LIMIT, SWIGLU_LIMIT)
        a = (up + 1.0) * (gate * (1.0 / (1.0 + jnp.exp(-SWIGLU_ALPHA * gate))))
        o_ref[...] = jnp.dot(a.astype(BF16), wd_b[...], preferred_element_type=F32) + bd_ref[...]

    @pl.when(i >= nu_ref[0])
    def _():
        o_ref[...] = jnp.zeros(o_ref.shape, F32)


def _moe(tile_e, n_used, xs, w_gate_up, b_gate_up, w_down, b_down, *, tmoe):
    n_rows, d = xs.shape
    n_tiles = n_rows // tmoe
    d_ff = w_down.shape[1]
    assert w_gate_up.shape[1] == d and w_down.shape[1] == w_down.shape[2] == d
    grid_spec = pltpu.PrefetchScalarGridSpec(
        num_scalar_prefetch=2,
        grid=(n_tiles,),
        in_specs=[pl.BlockSpec((tmoe, d), lambda i, te, nu: (i, 0)),
                  pl.BlockSpec((None, d, 2 * d_ff), lambda i, te, nu: (te[i], 0, 0)),
                  pl.BlockSpec((None, 1, 2 * d_ff), lambda i, te, nu: (te[i], 0, 0)),
                  pl.BlockSpec((None, d_ff, d), lambda i, te, nu: (te[i], 0, 0)),
                  pl.BlockSpec((None, 1, d), lambda i, te, nu: (te[i], 0, 0))],
        out_specs=pl.BlockSpec((tmoe, d), lambda i, te, nu: (i, 0)),
        scratch_shapes=[pltpu.VMEM((d, 2 * d_ff), BF16), pltpu.VMEM((d_ff, d), BF16)],
    )
    return pl.pallas_call(
        functools.partial(_moe_kernel, d_ff=d_ff),
        out_shape=jax.ShapeDtypeStruct((n_rows, d), F32),
        grid_spec=grid_spec,
        compiler_params=_cparams("arbitrary"),
        name="moe",
    )(tile_e, n_used, xs, w_gate_up, b_gate_up.reshape(N_EXPERTS, 1, -1), w_down,
      b_down.reshape(N_EXPERTS, 1, -1))


def _final_kernel(y4_ref, sw_ref, x1_ref, g2_ref, gpf_ref, o_ref):
    sw = sw_ref[...]
    f = ((y4_ref[0] * sw[:, 0:1] + y4_ref[1] * sw[:, 1:2]) + (y4_ref[2] * sw[:, 2:3] + y4_ref[3] * sw[:, 3:4]))
    o_ref[...] = x1_ref[...] + g2_ref[...] * _rms(f, gpf_ref[...])


def _final(y4, sw, x1, gate2, g_post_ffn, *, tm):
    t, d = x1.shape
    per_row = gate2.shape[0] != 1
    mod_spec = pl.BlockSpec((tm, d), lambda i: (i, 0)) if per_row else pl.BlockSpec((1, d), lambda i: (0, 0))
    return pl.pallas_call(
        _final_kernel,
        out_shape=jax.ShapeDtypeStruct((t, d), F32),
        grid=(t // tm,),
        in_specs=[pl.BlockSpec((TOP_K_EXPERTS, tm, d), lambda i: (0, i, 0)),
                  pl.BlockSpec((tm, LANES), lambda i: (i, 0)),
                  pl.BlockSpec((tm, d), lambda i: (i, 0)), mod_spec,
                  pl.BlockSpec((1, d), lambda i: (0, 0))],
        out_specs=pl.BlockSpec((tm, d), lambda i: (i, 0)),
        compiler_params=_cparams("arbitrary"),
        name="final",
    )(y4, sw, x1, gate2, g_post_ffn)


def _route(sel_e, tmoe):
    n_tok = sel_e.shape[0]
    n_assign = n_tok * TOP_K_EXPERTS
    onehot = sel_e[:, :, None] == jnp.arange(N_EXPERTS, dtype=I32)[None, None, :]
    per_tok = jnp.sum(onehot.astype(I32), axis=1)
    before = jnp.cumsum(per_tok, axis=0) - per_tok
    counts = before[-1] + per_tok[-1]
    padded = (counts + tmoe - 1) // tmoe * tmoe
    pad_end = jnp.cumsum(padded)
    pad_start = pad_end - padded
    dest = jnp.sum(jnp.where(onehot, (before + pad_start[None, :])[:, None, :], 0), axis=2)
    n_tiles = -(-n_assign // tmoe) + N_EXPERTS
    n_rows = n_tiles * tmoe
    flat_tok = jnp.repeat(jnp.arange(n_tok, dtype=I32), TOP_K_EXPERTS)
    row_tok = jnp.full((n_rows,), n_tok, I32).at[dest.reshape(-1)].set(flat_tok)
    tile_start = jnp.arange(n_tiles, dtype=I32) * tmoe
    tile_e = jnp.sum((tile_start[:, None] >= pad_end[None, :]).astype(I32), axis=1)
    n_used = (pad_end[-1] // tmoe).astype(I32).reshape(1)
    last_e = jnp.max(jnp.where(counts > 0, jnp.arange(N_EXPERTS, dtype=I32), 0))
    tile_e = jnp.where(tile_start < pad_end[-1], jnp.minimum(tile_e, N_EXPERTS - 1), last_e).astype(I32)
    return row_tok, tile_e, n_used, dest


def _pad_cols(a, n):
    return jnp.concatenate([a, jnp.zeros(a.shape[:-1] + (n - a.shape[-1],), a.dtype)], axis=-1)


def _head_pair_slabs(qt, tq):
    d, t = qt.shape
    hpg = MXU_DEPTH // HEAD_DIM
    shape = (N_HEADS, MXU_DEPTH, 1)
    rows = lax.broadcasted_iota(I32, shape, 1) + MXU_DEPTH * (lax.broadcasted_iota(I32, shape, 0) // hpg)
    own = (rows // HEAD_DIM) == lax.broadcasted_iota(I32, shape, 0)
    tiled = jnp.repeat(qt.reshape(d // MXU_DEPTH, MXU_DEPTH, t), hpg, axis=0)
    qz = jnp.where(own, tiled, jnp.zeros((), qt.dtype))
    nb = t // tq
    a = jnp.transpose(qz.reshape(N_HEADS // 2, 2, MXU_DEPTH, nb, tq), (0, 2, 3, 1, 4))
    return a.reshape(N_HEADS // 2, MXU_DEPTH, nb * 2 * tq)


def kernel(x_prompt, x_sample, c_prompt, c_sample, cache_k, cache_v, cache_kidx, page_table, w_ada, b_ada,
           g_pre_mix, w_in, gmlp_ln_g, gmlp_ln_b, gmlp_w_s, gmlp_b_s, w_out, g_post_mix, g_pre_ffn, w_router,
           b_router, w_gate_up, b_gate_up, w_down, b_down, g_post_ffn):
    depth = w_ada.shape[0]
    assert depth == 1 and x_prompt.shape[0] == 1 and x_sample.shape[1] == 1
    _, t, d = x_prompt.shape
    bd = x_sample.shape[0]
    n_pages = page_table.shape[1]
    n_past = n_pages * PAGE_SIZE
    d_attn = N_HEADS * HEAD_DIM
    l = 0
    row2 = lambda a: a.reshape(1, -1)

    c_all = jnp.concatenate([c_prompt, c_sample], axis=0)
    m_pad = -(-c_all.shape[0] // SUBLANES) * SUBLANES
    c_all = jnp.concatenate([c_all, jnp.zeros((m_pad - c_all.shape[0], d), F32)], axis=0)
    mod = _adaln(c_all, w_ada[l], b_ada[l])
    mod_p = [mod[0:1, i * d:(i + 1) * d] for i in range(6)]
    mod_s = [mod[1:1 + bd, i * d:(i + 1) * d] for i in range(6)]

    w_in_l = w_in[l]
    w_in_p = jnp.concatenate([w_in_l[:, :_C_KI], _pad_cols(w_in_l[:, 3072:3136], LANES),
                              _pad_cols(w_in_l[:, 3136:3144], LANES)], axis=1).astype(BF16)
    tril = jnp.tril(jnp.ones((CHUNK, CHUNK), dtype=bool))
    ws_chunk = jnp.where(tril[None], gmlp_w_s[l], 0.0).astype(BF16)
    gw = 512 // GMLP_GROUPS
    bs_chunk = jnp.repeat(jnp.transpose(gmlp_b_s[l]), gw, axis=1)
    ws_row = jnp.repeat(gmlp_w_s[l][:, 0, 0], gw).reshape(1, -1)
    bs_row = jnp.repeat(gmlp_b_s[l][:, 0], gw).reshape(1, -1)
    wo = w_out[l].astype(BF16)
    wo_a, wo_b = wo[:512], wo[512:]
    wr_p = _pad_cols(w_router[l], LANES).astype(BF16)
    br_p = jnp.concatenate([b_router[l], jnp.full((LANES - N_EXPERTS,), NEG_INF, F32)]).reshape(1, LANES)
    g_pre, g_pm, g_pf, g_po = row2(g_pre_mix[l]), row2(g_post_mix[l]), row2(g_pre_ffn[l]), row2(g_post_ffn[l])
    ln_g, ln_b = row2(gmlp_ln_g[l]), row2(gmlp_ln_b[l])

    xp = x_prompt[0]
    tq = 128
    (q_p, k_p, kb_p, v_p, vb_p, qi_p, ki_p, kib_p, wi_p, oa_p, _) = _premix(
        xp, mod_p[0], mod_p[1], g_pre, w_in_p, ln_g, ln_b, ws_chunk, bs_chunk, tm=256, chunked=True)
    ob_p = _dsa_prompt(_head_pair_slabs(jnp.transpose(q_p), tq), jnp.transpose(qi_p), jnp.transpose(wi_p),
                       kib_p, kb_p, jnp.transpose(vb_p), tq=tq, tk=min(512, t), tk3=min(512, t))
    x1_p, h2_p, se_p, sw_p = _postmix(oa_p, ob_p, wo_a, wo_b, xp, mod_p[2], mod_p[3], mod_p[4], g_pm, g_pf,
                                      wr_p, br_p, tm=256)

    xs_ = x_sample[:, 0]
    (q_s, k_s, _, v_s, _, qi_s, ki_s, _, wi_s, oa_s, vn_s) = _premix(
        xs_, mod_s[0], mod_s[1], g_pre, w_in_p, ln_g, ln_b, ws_row, bs_row, tm=bd, chunked=False)
    page_flat = page_table.reshape(-1)
    pg = 16 if n_pages % 16 == 0 else 1
    scores, snew = _idx_sample(page_flat, qi_s.reshape(bd, IDX_HEADS, IDX_DIM), wi_s.reshape(bd, IDX_HEADS, 1),
                               ki_s.reshape(bd, 1, IDX_DIM), cache_kidx[l], n_pages=n_pages, pg=pg)
    topk_s = min(TOPK_MAX, (n_past + 1) // 4)
    mask, mnew = _topk_sample(scores, snew, topk_s)
    mask_flat = jnp.concatenate([mask.reshape(bd, n_past), mnew[:, 0, :1]], axis=1)
    _, sel = lax.top_k(mask_flat, topk_s)
    is_new = sel >= n_past
    newpos = jnp.where(jnp.any(is_new, axis=1), jnp.argmax(is_new, axis=1), -1).astype(I32)
    sp = jnp.minimum(sel, n_past - 1)
    phys = jnp.take_along_axis(page_table, sp // PAGE_SIZE, axis=1)
    slab = (phys * PAGE_SIZE + sp % PAGE_SIZE).astype(I32).reshape(-1)
    hd = (bd, N_HEADS, HEAD_DIM)
    ob_s = _attn_gather(slab, newpos, q_s.reshape(hd), k_s.reshape(hd), v_s.reshape(hd),
                        cache_k[l].reshape(-1, HEAD_DIM), cache_v[l].reshape(-1, HEAD_DIM), topk=topk_s)
    x1_s, h2_s, se_s, sw_s = _postmix(oa_s, ob_s.reshape(bd, d_attn).astype(BF16), wo_a, wo_b, xs_, mod_s[2],
                                      mod_s[3], mod_s[4], g_pm, g_pf, wr_p, br_p, tm=bd)

    tmoe = 256
    h2_all = jnp.concatenate([h2_p, h2_s, jnp.zeros((1, d), BF16)], axis=0)
    sel_e = jnp.concatenate([se_p[:, :TOP_K_EXPERTS], se_s[:, :TOP_K_EXPERTS]], axis=0)
    row_tok, tile_e, n_used, dest = _route(sel_e, tmoe)
    ys = _moe(tile_e, n_used, h2_all[row_tok], w_gate_up[l], b_gate_up[l], w_down[l], b_down[l], tmoe=tmoe)
    y_p = _final(ys[jnp.transpose(dest[:t])], sw_p, x1_p, mod_p[5], g_po, tm=256)
    y_s = _final(ys[jnp.transpose(dest[t:])], sw_s, x1_s, mod_s[5], g_po, tm=bd)

    hs = (N_HEADS, HEAD_DIM)
    return (y_p[None], y_s[:, None],
            k_p.reshape(1, 1, t, *hs), v_p.reshape(1, 1, t, *hs), ki_p.reshape(1, 1, t, IDX_DIM),
            k_s.reshape(1, bd, 1, *hs), v_s.reshape(1, bd, 1, *hs), ki_s.reshape(1, bd, 1, IDX_DIM),
            vn_s.reshape(1, bd, 1, -1))
```

```python
import functools
import math

import jax
import jax.numpy as jnp
from jax import lax
from jax.experimental import pallas as pl
from jax.experimental.pallas import tpu as pltpu

F32 = jnp.float32
BF16 = jnp.bfloat16
I32 = jnp.int32

EPS = 1e-6
N_HEADS = 8
HEAD_DIM = 64
IDX_HEADS = 8
IDX_DIM = 64
GMLP_GROUPS = 4
CHUNK = 128
TOPK_MAX = 256
PAGE_SIZE = 128
N_EXPERTS = 32
TOP_K_EXPERTS = 4
SWIGLU_LIMIT = 7.0
SWIGLU_ALPHA = 1.702
LOG2E = 1.4426950408889634

LANES = 128
SUBLANES = 8
MXU_DEPTH = 256
VMEM_LIMIT = 60000 * 1024
INT_MIN = -2147483648
INT_MAX = 2147483647
NEG_INF = float("-inf")


def _cparams(*sem):
    return pltpu.CompilerParams(dimension_semantics=sem, vmem_limit_bytes=VMEM_LIMIT)


def _gelu_tanh(x):
    return 0.5 * x * (1.0 + jnp.tanh(0.7978845608028654 * (x + 0.044715 * (x * x * x))))


def _rms(x, g):
    return x * lax.rsqrt(jnp.mean(x * x, axis=-1, keepdims=True) + EPS) * g


def _ordered_bits_to_float(u):
    bits = jnp.where(u >= 0, u, u ^ jnp.int32(0x7FFFFFFF))
    return lax.bitcast_convert_type(bits, F32)


def _adaln_kernel(c_ref, w_ref, b_ref, o_ref):
    c = c_ref[...]
    a = c * (1.0 / (1.0 + jnp.exp(-c)))
    o_ref[...] = jnp.dot(a.astype(BF16), w_ref[...].astype(BF16), preferred_element_type=F32) + b_ref[...]


def _adaln(c_all, w_ada, b_ada):
    m, d = c_all.shape
    n = w_ada.shape[1]
    tn = 1536
    return pl.pallas_call(
        _adaln_kernel,
        out_shape=jax.ShapeDtypeStruct((m, n), F32),
        grid=(n // tn,),
        in_specs=[pl.BlockSpec((m, d), lambda j: (0, 0)),
                  pl.BlockSpec((d, tn), lambda j: (0, j)),
                  pl.BlockSpec((1, tn), lambda j: (0, j))],
        out_specs=pl.BlockSpec((m, tn), lambda j: (0, j)),
        compiler_params=_cparams("arbitrary"),
        name="adaln",
    )(c_all, w_ada, b_ada.reshape(1, n))


_C_U, _C_V, _C_Q, _C_K, _C_VV, _C_QI, _C_KI, _C_WI, _C_END = 0, 512, 1024, 1536, 2048, 2560, 3072, 3200, 3328


def _premix_kernel(x_ref, sh_ref, sc_ref, g_ref, w_ref, lg_ref, lb_ref, ws_ref, bs_ref,
                   q_ref, k_ref, kb_ref, v_ref, vb_ref, qi_ref, ki_ref, kib_ref, wi_ref, oa_ref, vn_ref,
                   *, tm, chunked):
    x = x_ref[...]
    h = _rms(x, g_ref[...]) * (1.0 + sc_ref[...]) + sh_ref[...]
    p = jnp.dot(h.astype(BF16), w_ref[...], preferred_element_type=F32)
    k = p[:, _C_K:_C_VV]
    vv = p[:, _C_VV:_C_QI]
    ki = p[:, _C_KI:_C_KI + IDX_DIM]
    q_ref[...] = (p[:, _C_Q:_C_K] * (HEAD_DIM ** -0.5 * LOG2E)).astype(BF16)
    k_ref[...] = k
    kb_ref[...] = k.astype(BF16)
    v_ref[...] = vv
    vb_ref[...] = vv.astype(BF16)
    qi_ref[...] = (p[:, _C_QI:_C_KI] * (IDX_DIM ** -0.5)).astype(BF16)
    ki_ref[...] = ki
    kib_ref[...] = ki.astype(BF16)
    wi_ref[...] = p[:, _C_WI:_C_WI + IDX_HEADS] * (IDX_HEADS ** -0.5)
    gu = _gelu_tanh(p[:, _C_U:_C_V])
    gv = _gelu_tanh(p[:, _C_V:_C_Q])
    mu = jnp.mean(gv, axis=-1, keepdims=True)
    dv = gv - mu
    var = jnp.mean(dv * dv, axis=-1, keepdims=True)
    vn = dv * lax.rsqrt(var + EPS) * lg_ref[...] + lb_ref[...]
    vn_ref[...] = vn
    if chunked:
        gw = vn.shape[1] // GMLP_GROUPS
        vnb = vn.astype(BF16)
        for r in range(tm // CHUNK):
            rs = slice(r * CHUNK, (r + 1) * CHUNK)
            for g in range(GMLP_GROUPS):
                cs = slice(g * gw, (g + 1) * gw)
                s = jnp.dot(ws_ref[g], vnb[rs, cs], preferred_element_type=F32) + bs_ref[:, cs]
                oa_ref[rs, cs] = (gu[rs, cs] * s).astype(BF16)
    else:
        oa_ref[...] = (gu * (vn * ws_ref[...] + bs_ref[...])).astype(BF16)


def _premix(x, shift, scale, g_pre, w_in_p, ln_g, ln_b, ws, bs, *, tm, chunked):
    t, d = x.shape
    per_row = shift.shape[0] != 1
    mod_spec = pl.BlockSpec((tm, d), lambda i: (i, 0)) if per_row else pl.BlockSpec((1, d), lambda i: (0, 0))
    const2 = lambda a: pl.BlockSpec(a.shape, lambda i: (0, 0))
    ws_spec = pl.BlockSpec(ws.shape, lambda i: (0, 0, 0)) if chunked else const2(ws)
    row = lambda n: pl.BlockSpec((tm, n), lambda i: (i, 0))
    outs = [((t, 512), BF16), ((t, 512), F32), ((t, 512), BF16), ((t, 512), F32), ((t, 512), BF16),
            ((t, 512), BF16), ((t, IDX_DIM), F32), ((t, IDX_DIM), BF16), ((t, IDX_HEADS), F32),
            ((t, 512), BF16), ((t, 512), F32)]
    return pl.pallas_call(
        functools.partial(_premix_kernel, tm=tm, chunked=chunked),
        out_shape=[jax.ShapeDtypeStruct(s, dt) for s, dt in outs],
        grid=(t // tm,),
        in_specs=[row(d), mod_spec, mod_spec, const2(g_pre), const2(w_in_p), const2(ln_g), const2(ln_b),
                  ws_spec, const2(bs)],
        out_specs=[row(s[1]) for s, _ in outs],
        compiler_params=_cparams("arbitrary"),
        name="premix_chunked" if chunked else "premix_rows",
    )(x, shift, scale, g_pre, w_in_p, ln_g, ln_b, ws, bs)


_CNT_ROWS = 64
_RED_ROWS = 32


def _dsa_prompt_kernel(qz2_ref, qit_ref, wt_ref, kidx_ref, k_ref, vt_ref, o_ref,
                       s_ref, m_ref, l_ref, acc_ref, lg_ref, p_ref, *, tq, tk, tk3, topk, nbits):
    q0 = pl.program_id(0) * tq
    n_chunks = (q0 + tq + tk - 1) // tk
    n_chunks3 = (q0 + tq + tk3 - 1) // tk3
    qpos = q0 + lax.broadcasted_iota(I32, (1, tq), 1)
    w = wt_ref[...]

    def key_pos(c, size):
        return c * size + lax.broadcasted_iota(I32, (size, 1), 0)

    def p1(c, carry):
        k0 = pl.multiple_of(c * tk, tk)
        kc = kidx_ref[pl.ds(k0, tk), :]
        acc = jnp.zeros((tk, tq), F32)
        for h in range(IDX_HEADS):
            d = jnp.dot(kc, qit_ref[h * IDX_DIM:(h + 1) * IDX_DIM, :], preferred_element_type=F32)
            acc = acc + w[h:h + 1, :] * jnp.maximum(d, 0.0)
        s_ref[pl.ds(k0, tk), :] = jnp.where(key_pos(c, tk) <= qpos, acc, NEG_INF)
        return carry

    lax.fori_loop(0, n_chunks, p1, 0)

    def count(pred):
        def body(c, acc):
            k0 = pl.multiple_of(c * tk, tk)
            hit = jnp.where(pred(s_ref[pl.ds(k0, tk), :], c), 1.0, 0.0)
            return acc + jnp.sum(hit.reshape(tk // _CNT_ROWS, _CNT_ROWS, tq), axis=0)
        acc = lax.fori_loop(0, n_chunks, body, jnp.zeros((_CNT_ROWS, tq), F32))
        return jnp.sum(acc, axis=0, keepdims=True)

    kf = float(topk)
    select_all = (qpos + 1) <= topk
    cnt0 = count(lambda s, c: s >= 0.0)
    nonneg = cnt0 >= kf
    base0 = jnp.where(nonneg, jnp.int32(0), jnp.int32(INT_MIN))
    cntb0 = jnp.where(nonneg, cnt0, (qpos + 1).astype(F32))

    def bisect(b, state):
        base, cntb = state
        cand = base | jnp.left_shift(jnp.int32(1), 30 - b)
        thr = _ordered_bits_to_float(cand)
        c = count(lambda s, cc: s >= thr)
        ok = c >= kf
        return jnp.where(ok, cand, base), jnp.where(ok, c, cntb)

    base, cntb = lax.fori_loop(0, 31, bisect, (base0, cntb0))
    tau = jnp.where(select_all, NEG_INF, _ordered_bits_to_float(base))
    straddle = jnp.logical_and(cntb > kf, jnp.logical_not(select_all))

    def tie_search():
        need = kf - count(lambda s, c: s > tau)

        def step(b, j):
            cand = j | jnp.left_shift(jnp.int32(1), nbits - 1 - b)
            below = count(lambda s, c: jnp.logical_and(s == tau, key_pos(c, tk) < cand))
            return jnp.where(below < need, cand, j)
        return lax.fori_loop(0, nbits, step, jnp.zeros((1, tq), I32))

    any_straddle = jnp.max(jnp.where(straddle, 1, 0)) > 0
    j_tie = lax.cond(any_straddle, tie_search, lambda: jnp.zeros((1, tq), I32))
    j_tie = jnp.where(straddle, j_tie, jnp.int32(INT_MAX))

    m_ref[...] = jnp.full(m_ref.shape, NEG_INF, F32)
    l_ref[...] = jnp.zeros(l_ref.shape, F32)
    acc_ref[...] = jnp.zeros(acc_ref.shape, F32)
    hpg = MXU_DEPTH // HEAD_DIM
    col = lambda x, op: op(op(x.reshape(tk3 // _RED_ROWS, _RED_ROWS, tq), axis=0), axis=0, keepdims=True)

    def chunk_start(c):
        return pl.multiple_of(jnp.minimum(c, n_chunks3 - 1) * tk3, tk3)

    def logits(c, buf):
        k0 = chunk_start(c)
        s = s_ref[pl.ds(k0, tk3), :]
        kp = k0 + lax.broadcasted_iota(I32, (tk3, 1), 0)
        sel = jnp.logical_or(s > tau, jnp.logical_and(s == tau, kp <= j_tie))
        live = jnp.logical_and(kp <= qpos, c < n_chunks3)
        bias = jnp.where(jnp.logical_and(sel, live), 0.0, NEG_INF)
        cmax = []
        for pr in range(N_HEADS // 2):
            g = (2 * pr) // hpg
            kc = k_ref[pl.ds(k0, tk3), g * MXU_DEPTH:(g + 1) * MXU_DEPTH]
            lg2 = jnp.dot(kc, qz2_ref[pr], preferred_element_type=F32)
            for half in range(2):
                lg = lg2[:, half * tq:(half + 1) * tq] + bias
                lg_ref[buf, 2 * pr + half] = lg
                cmax.append(col(lg, jnp.max))
        return jnp.concatenate(cmax, axis=0)

    def softmax_pv(c, buf, cmax):
        k0 = chunk_start(c)
        m_old = m_ref[...]
        m_new = jnp.maximum(m_old, cmax)
        m_safe = jnp.where(m_new == NEG_INF, 0.0, m_new)
        alpha = jnp.exp2(m_old - m_safe)
        m_ref[...] = m_new
        psum = []
        for h in range(N_HEADS):
            p = jnp.exp2(lg_ref[buf, h] - m_safe[h:h + 1, :])
            p_ref[buf, h] = p.astype(BF16)
            psum.append(col(p, jnp.sum))
        l_ref[...] = alpha * l_ref[...] + jnp.concatenate(psum, axis=0)
        for h in range(N_HEADS):
            hs = slice(h * HEAD_DIM, (h + 1) * HEAD_DIM)
            pv = jnp.dot(vt_ref[hs, pl.ds(k0, tk3)], p_ref[buf, h], preferred_element_type=F32)
            acc_ref[hs, :] = alpha[h:h + 1, :] * acc_ref[hs, :] + pv

    def p3(j, cmax_even):
        cmax_odd = logits(2 * j + 1, 1)
        softmax_pv(2 * j, 0, cmax_even)
        cmax_even = logits(2 * j + 2, 0)
        softmax_pv(2 * j + 1, 1, cmax_odd)
        return cmax_even

    lax.fori_loop(0, (n_chunks3 + 1) // 2, p3, logits(0, 0))
    for h in range(N_HEADS):
        hs = slice(h * HEAD_DIM, (h + 1) * HEAD_DIM)
        acc_ref[hs, :] = acc_ref[hs, :] * (1.0 / l_ref[h:h + 1, :])
    o_ref[...] = jnp.transpose(acc_ref[...]).astype(BF16)


def _dsa_prompt(qz2, qit, wt, kidx_b, k_b, vt_b, *, tq, tk, tk3):
    t = k_b.shape[0]
    topk = min(TOPK_MAX, t // 4)
    nbits = max(1, math.ceil(math.log2(t)))
    resident = lambda a: pl.BlockSpec(a.shape, lambda i: (0,) * a.ndim, pipeline_mode=pl.Buffered(1))
    return pl.pallas_call(
        functools.partial(_dsa_prompt_kernel, tq=tq, tk=tk, tk3=tk3, topk=topk, nbits=nbits),
        out_shape=jax.ShapeDtypeStruct((t, N_HEADS * HEAD_DIM), BF16),
        grid=(t // tq,),
        in_specs=[pl.BlockSpec((N_HEADS // 2, MXU_DEPTH, 2 * tq), lambda i: (0, 0, i)),
                  pl.BlockSpec((IDX_HEADS * IDX_DIM, tq), lambda i: (0, i)),
                  pl.BlockSpec((IDX_HEADS, tq), lambda i: (0, i)),
                  resident(kidx_b), resident(k_b), resident(vt_b)],
        out_specs=pl.BlockSpec((tq, N_HEADS * HEAD_DIM), lambda i: (i, 0)),
        scratch_shapes=[pltpu.VMEM((t, tq), F32), pltpu.VMEM((N_HEADS, tq), F32),
                        pltpu.VMEM((N_HEADS, tq), F32), pltpu.VMEM((N_HEADS * HEAD_DIM, tq), F32),
                        pltpu.VMEM((2, N_HEADS, tk3, tq), F32), pltpu.VMEM((2, N_HEADS, tk3, tq), BF16)],
        compiler_params=_cparams("arbitrary"),
        name="dsa_prompt",
    )(qz2, qit, wt, kidx_b, k_b, vt_b)


def _idx_sample_kernel(pt_ref, q_ref, w_ref, kn_ref, *rest, pg):
    page_refs, s_ref, sn_ref = rest[:pg], rest[pg], rest[pg + 1]
    q = q_ref[...]
    w = w_ref[...]
    for r in range(pg):
        kp = page_refs[r][...].astype(BF16)
        d = lax.dot_general(q, kp, (((1,), (1,)), ((), ())), preferred_element_type=F32)
        s_ref[r:r + 1, :] = jnp.sum(w * jnp.maximum(d, 0.0), axis=0, keepdims=True)
    kn = kn_ref[...].astype(BF16).astype(F32)
    dn = jnp.sum(q.astype(F32) * kn, axis=1, keepdims=True)
    sn = jnp.sum(w * jnp.maximum(dn, 0.0), axis=0, keepdims=True)
    sn_ref[...] = jnp.broadcast_to(sn, sn_ref.shape)


def _idx_sample(page_flat, qi_s, w_s, kidx_new, cache_kidx, layer, *, n_pages, pg):
    bd = qi_s.shape[0]
    page_spec = lambda r: pl.BlockSpec((None, None, PAGE_SIZE, IDX_DIM),
                                       lambda b, j, pt: (layer, pt[b * n_pages + j * pg + r], 0, 0))
    grid_spec = pltpu.PrefetchScalarGridSpec(
        num_scalar_prefetch=1,
        grid=(bd, n_pages // pg),
        in_specs=[pl.BlockSpec((None, IDX_HEADS, IDX_DIM), lambda b, j, pt: (b, 0, 0)),
                  pl.BlockSpec((None, IDX_HEADS, 1), lambda b, j, pt: (b, 0, 0)),
                  pl.BlockSpec((None, 1, IDX_DIM), lambda b, j, pt: (b, 0, 0))]
                 + [page_spec(r) for r in range(pg)],
        out_specs=[pl.BlockSpec((None, pg, PAGE_SIZE), lambda b, j, pt: (b, j, 0)),
                   pl.BlockSpec((None, 1, LANES), lambda b, j, pt: (b, 0, 0))],
    )
    return pl.pallas_call(
        functools.partial(_idx_sample_kernel, pg=pg),
        out_shape=[jax.ShapeDtypeStruct((bd, n_pages, PAGE_SIZE), F32),
                   jax.ShapeDtypeStruct((bd, 1, LANES), F32)],
        grid_spec=grid_spec,
        compiler_params=_cparams("arbitrary", "arbitrary"),
        name="idx_sample",
    )(page_flat, qi_s, w_s, kidx_new, *([cache_kidx] * pg))


def _topk_sample_kernel(s_ref, sn_ref, sel_ref, *, topk, nbits):
    s = s_ref[...]
    sn = sn_ref[...][:, 0:1]
    n_pg, width = s.shape
    n_past = n_pg * width
    kpos = lax.broadcasted_iota(I32, s.shape, 0) * width + lax.broadcasted_iota(I32, s.shape, 1)
    kf = float(topk)

    def count(pred_past, pred_new):
        c = jnp.sum(jnp.sum(jnp.where(pred_past, 1.0, 0.0), axis=0, keepdims=True), axis=1, keepdims=True)
        return c + jnp.where(pred_new, 1.0, 0.0)

    cnt0 = count(s >= 0.0, sn >= 0.0)
    base0 = jnp.where(cnt0 >= kf, jnp.int32(0), jnp.int32(INT_MIN))

    def bisect(b, base):
        cand = base | jnp.left_shift(jnp.int32(1), 30 - b)
        thr = _ordered_bits_to_float(cand)
        return jnp.where(count(s >= thr, sn >= thr) >= kf, cand, base)

    tau = _ordered_bits_to_float(lax.fori_loop(0, 31, bisect, base0))
    need = kf - count(s > tau, sn > tau)

    def step(b, j):
        cand = j | jnp.left_shift(jnp.int32(1), nbits - 1 - b)
        below = count(jnp.logical_and(s == tau, kpos < cand), jnp.logical_and(sn == tau, n_past < cand))
        return jnp.where(below < need, cand, j)

    j_tie = lax.fori_loop(0, nbits, step, jnp.zeros((1, 1), I32))
    sel = jnp.logical_or(s > tau, jnp.logical_and(s == tau, kpos <= j_tie))

    mask = jnp.where(sel, 1.0, 0.0).astype(BF16)
    nt = (((1,), (1,)), ((), ()))
    per_page = lax.dot_general(jnp.ones((SUBLANES, width), BF16), mask, nt, preferred_element_type=F32)
    incl_tri = lax.broadcasted_iota(I32, (n_pg, n_pg), 0) <= lax.broadcasted_iota(I32, (n_pg, n_pg), 1)
    upto = jnp.dot(per_page.astype(BF16), jnp.where(incl_tri, 1.0, 0.0).astype(BF16),
                   preferred_element_type=F32)[0:1, :]
    per_page = per_page[0:1, :]
    rank = lax.broadcasted_iota(I32, (topk, 1), 0).astype(F32)
    done = upto <= rank
    page = jnp.sum(jnp.where(done, 1.0, 0.0), axis=1, keepdims=True)
    before = jnp.sum(jnp.where(done, per_page, 0.0), axis=1, keepdims=True)
    pick = jnp.where(lax.broadcasted_iota(I32, (topk, n_pg), 1).astype(F32) == page, 1.0, 0.0).astype(BF16)
    strict_tri = lax.broadcasted_iota(I32, (width, width), 0) < lax.broadcasted_iota(I32, (width, width), 1)
    prefix = jnp.dot(mask, jnp.where(strict_tri, 1.0, 0.0).astype(BF16), preferred_element_type=F32)
    prefix_row = jnp.dot(pick, prefix.astype(BF16), preferred_element_type=F32)
    mask_row = jnp.dot(pick, mask, preferred_element_type=F32)
    hit = jnp.logical_and(prefix_row == rank - before, mask_row > 0.5)
    lane = jnp.sum(jnp.where(hit, lax.broadcasted_iota(I32, (topk, width), 1).astype(F32), 0.0),
                   axis=1, keepdims=True)
    sel_ref[...] = (page * float(width) + lane).astype(I32)


def _topk_sample(scores, snew, topk):
    bd, n_pages, _ = scores.shape
    nbits = max(1, math.ceil(math.log2(n_pages * PAGE_SIZE + 1)))
    return pl.pallas_call(
        functools.partial(_topk_sample_kernel, topk=topk, nbits=nbits),
        out_shape=jax.ShapeDtypeStruct((bd, topk, 1), I32),
        grid=(bd,),
        in_specs=[pl.BlockSpec((None, n_pages, PAGE_SIZE), lambda b: (b, 0, 0)),
                  pl.BlockSpec((None, 1, LANES), lambda b: (b, 0, 0))],
        out_specs=pl.BlockSpec((None, topk, 1), lambda b: (b, 0, 0)),
        compiler_params=_cparams("arbitrary"),
        name="topk_sample",
    )(scores, snew)


def _attn_gather_kernel(phys_ref, row_ref, newpos_ref, q_ref, kn_ref, vn_ref, ck_ref, cv_ref, o_ref,
                        kbuf, vbuf, sem, *, topk, layer):
    b = pl.program_id(0)
    slot = b % 2

    def row_copy(src, buf, which, bb, sl, i):
        d0 = pl.multiple_of(i * N_HEADS, N_HEADS)
        return pltpu.make_async_copy(src.at[layer, phys_ref[bb * topk + i], row_ref[bb * topk + i]],
                                     buf.at[sl, pl.ds(d0, N_HEADS), :], sem.at[which, sl])

    def for_rows(bb, sl, fn):
        def body(i, carry):
            fn(row_copy(ck_ref, kbuf, 0, bb, sl, i))
            fn(row_copy(cv_ref, vbuf, 1, bb, sl, i))
            return carry
        lax.fori_loop(0, topk, body, 0)

    @pl.when(b == 0)
    def _():
        for_rows(0, 0, lambda cp: cp.start())

    @pl.when(b + 1 < pl.num_programs(0))
    def _():
        for_rows(b + 1, 1 - slot, lambda cp: cp.start())

    for_rows(b, slot, lambda cp: cp.wait())

    @pl.when(newpos_ref[b] >= 0)
    def _():
        d0 = pl.multiple_of(newpos_ref[b] * N_HEADS, N_HEADS)
        kbuf[slot, pl.ds(d0, N_HEADS), :] = kn_ref[...]
        vbuf[slot, pl.ds(d0, N_HEADS), :] = vn_ref[...]

    kb = kbuf[slot].reshape(topk, N_HEADS, HEAD_DIM)
    vb = vbuf[slot].reshape(topk, N_HEADS, HEAD_DIM)
    q = q_ref[...].astype(F32)
    lg = jnp.sum(kb * q[None], axis=2, keepdims=True)
    m = jnp.max(lg, axis=0, keepdims=True)
    p = jnp.exp2(lg - m)
    l = jnp.sum(p, axis=0, keepdims=True)
    o_ref[...] = jnp.sum(p * vb, axis=0) * (1.0 / l[0])


def _attn_gather(phys, row, newpos, q_s, k_new, v_new, cache_k, cache_v, layer, *, topk):
    bd = q_s.shape[0]
    per_b = pl.BlockSpec((None, N_HEADS, HEAD_DIM), lambda b, *_: (b, 0, 0))
    grid_spec = pltpu.PrefetchScalarGridSpec(
        num_scalar_prefetch=3,
        grid=(bd,),
        in_specs=[per_b, per_b, per_b, pl.BlockSpec(memory_space=pl.ANY), pl.BlockSpec(memory_space=pl.ANY)],
        out_specs=per_b,
        scratch_shapes=[pltpu.VMEM((2, topk * N_HEADS, HEAD_DIM), F32),
                        pltpu.VMEM((2, topk * N_HEADS, HEAD_DIM), F32),
                        pltpu.SemaphoreType.DMA((2, 2))],
    )
    return pl.pallas_call(
        functools.partial(_attn_gather_kernel, topk=topk, layer=layer),
        out_shape=jax.ShapeDtypeStruct((bd, N_HEADS, HEAD_DIM), F32),
        grid_spec=grid_spec,
        compiler_params=_cparams("arbitrary"),
        name="attn_gather",
    )(phys, row, newpos, q_s, k_new, v_new, cache_k, cache_v)


def _postmix_kernel(oa_ref, ob_ref, woa_ref, wob_ref, x_ref, g1_ref, sh_ref, sc_ref, gpm_ref, gpf_ref,
                    wr_ref, br_ref, x1_ref, h2_ref, se_ref, sw_ref):
    mix = (jnp.dot(oa_ref[...], woa_ref[...], preferred_element_type=F32)
           + jnp.dot(ob_ref[...], wob_ref[...], preferred_element_type=F32))
    x1 = x_ref[...] + g1_ref[...] * _rms(mix, gpm_ref[...])
    x1_ref[...] = x1
    h2 = _rms(x1, gpf_ref[...]) * (1.0 + sc_ref[...]) + sh_ref[...]
    h2b = h2.astype(BF16)
    h2_ref[...] = h2
    logits = jnp.dot(h2b, wr_ref[...], preferred_element_type=F32) + br_ref[...]
    lane = lax.broadcasted_iota(I32, logits.shape, 1)
    lane_f = lane.astype(F32)
    se = jnp.zeros(logits.shape, F32)
    sw = jnp.zeros(logits.shape, F32)
    top = None
    denom = None
    for r in range(TOP_K_EXPERTS):
        m = jnp.max(logits, axis=1, keepdims=True)
        idx = jnp.min(jnp.where(logits == m, lane_f, float(LANES)), axis=1, keepdims=True)
        if r == 0:
            top = m
        e = jnp.exp(m - top)
        denom = e if r == 0 else denom + e
        se = jnp.where(lane == r, idx, se)
        sw = jnp.where(lane == r, e, sw)
        logits = jnp.where(lane_f == idx, NEG_INF, logits)
    se_ref[...] = se.astype(I32)
    sw_ref[...] = sw * (1.0 / denom)


def _postmix(out_a, out_b, wo_a, wo_b, x, gate1, shift2, scale2, g_pm, g_pf, wr_p, br_p, *, tm):
    t, d = x.shape
    per_row = gate1.shape[0] != 1
    mod_spec = pl.BlockSpec((tm, d), lambda i: (i, 0)) if per_row else pl.BlockSpec((1, d), lambda i: (0, 0))
    const2 = lambda a: pl.BlockSpec(a.shape, lambda i: (0, 0))
    row = lambda n: pl.BlockSpec((tm, n), lambda i: (i, 0))
    return pl.pallas_call(
        _postmix_kernel,
        out_shape=[jax.ShapeDtypeStruct((t, d), F32), jax.ShapeDtypeStruct((t, d), F32),
                   jax.ShapeDtypeStruct((t, LANES), I32), jax.ShapeDtypeStruct((t, LANES), F32)],
        grid=(t // tm,),
        in_specs=[row(out_a.shape[1]), row(out_b.shape[1]), const2(wo_a), const2(wo_b), row(d),
                  mod_spec, mod_spec, mod_spec, const2(g_pm), const2(g_pf), const2(wr_p), const2(br_p)],
        out_specs=[row(d), row(d), row(LANES), row(LANES)],
        compiler_params=_cparams("arbitrary"),
        name="postmix",
    )(out_a, out_b, wo_a, wo_b, x, gate1, shift2, scale2, g_pm, g_pf, wr_p, br_p)


def _moe_kernel(te_ref, nu_ref, x_ref, wgu_ref, bgu_ref, wd_ref, bd_ref, o_ref, wgu_b, wd_b, *, d_ff):
    i = pl.program_id(0)
    changed = jnp.logical_or(i == 0, te_ref[i] != te_ref[jnp.maximum(i - 1, 0)])

    @pl.when(changed)
    def _():
        rows = 128
        def cast(r, carry):
            r0 = pl.multiple_of(r * rows, rows)
            wgu_b[pl.ds(r0, rows), :] = wgu_ref[pl.ds(r0, rows), :].astype(BF16)
            wd_b[pl.ds(r0, rows), :] = wd_ref[pl.ds(r0, rows), :].astype(BF16)
            return carry
        lax.fori_loop(0, wgu_b.shape[0] // rows, cast, 0)

    @pl.when(i < nu_ref[0])
    def _():
        gu = jnp.dot(x_ref[...].astype(BF16), wgu_b[...], preferred_element_type=F32) + bgu_ref[...]
        gate = jnp.minimum(gu[:, :d_ff], SWIGLU_LIMIT)
        up = jnp.clip(gu[:, d_ff:], -SWIGLU_LIMIT, SWIGLU_LIMIT)
        a = (up + 1.0) * (gate * (1.0 / (1.0 + jnp.exp(-SWIGLU_ALPHA * gate))))
        o_ref[...] = jnp.dot(a.astype(BF16), wd_b[...], preferred_element_type=F32) + bd_ref[...]

    @pl.when(i >= nu_ref[0])
    def _():
        o_ref[...] = jnp.zeros(o_ref.shape, F32)


def _moe(tile_e, n_used, xs, w_gate_up, b_gate_up, w_down, b_down, *, tmoe):
    n_rows, d = xs.shape
    n_tiles = n_rows // tmoe
    d_ff = w_down.shape[1]
    assert w_gate_up.shape[1] == d and w_down.shape[1] == w_down.shape[2] == d
    grid_spec = pltpu.PrefetchScalarGridSpec(
        num_scalar_prefetch=2,
        grid=(n_tiles,),
        in_specs=[pl.BlockSpec((tmoe, d), lambda i, te, nu: (i, 0)),
                  pl.BlockSpec((None, d, 2 * d_ff), lambda i, te, nu: (te[i], 0, 0)),
                  pl.BlockSpec((None, 1, 2 * d_ff), lambda i, te, nu: (te[i], 0, 0)),
                  pl.BlockSpec((None, d_ff, d), lambda i, te, nu: (te[i], 0, 0)),
                  pl.BlockSpec((None, 1, d), lambda i, te, nu: (te[i], 0, 0))],
        out_specs=pl.BlockSpec((tmoe, d), lambda i, te, nu: (i, 0)),
        scratch_shapes=[pltpu.VMEM((d, 2 * d_ff), BF16), pltpu.VMEM((d_ff, d), BF16)],
    )
    return pl.pallas_call(
        functools.partial(_moe_kernel, d_ff=d_ff),
        out_shape=jax.ShapeDtypeStruct((n_rows, d), F32),
        grid_spec=grid_spec,
        compiler_params=_cparams("arbitrary"),
        name="moe",
    )(tile_e, n_used, xs, w_gate_up, b_gate_up.reshape(N_EXPERTS, 1, -1), w_down,
      b_down.reshape(N_EXPERTS, 1, -1))


def _final_kernel(y4_ref, sw_ref, x1_ref, g2_ref, gpf_ref, o_ref):
    sw = sw_ref[...]
    f = ((y4_ref[0] * sw[:, 0:1] + y4_ref[1] * sw[:, 1:2]) + (y4_ref[2] * sw[:, 2:3] + y4_ref[3] * sw[:, 3:4]))
    o_ref[...] = x1_ref[...] + g2_ref[...] * _rms(f, gpf_ref[...])


def _final(y4, sw, x1, gate2, g_post_ffn, *, tm):
    t, d = x1.shape
    per_row = gate2.shape[0] != 1
    mod_spec = pl.BlockSpec((tm, d), lambda i: (i, 0)) if per_row else pl.BlockSpec((1, d), lambda i: (0, 0))
    return pl.pallas_call(
        _final_kernel,
        out_shape=jax.ShapeDtypeStruct((t, d), F32),
        grid=(t // tm,),
        in_specs=[pl.BlockSpec((TOP_K_EXPERTS, tm, d), lambda i: (0, i, 0)),
                  pl.BlockSpec((tm, LANES), lambda i: (i, 0)),
                  pl.BlockSpec((tm, d), lambda i: (i, 0)), mod_spec,
                  pl.BlockSpec((1, d), lambda i: (0, 0))],
        out_specs=pl.BlockSpec((tm, d), lambda i: (i, 0)),
        compiler_params=_cparams("arbitrary"),
        name="final",
    )(y4, sw, x1, gate2, g_post_ffn)


def _route(sel_e, tmoe):
    n_tok = sel_e.shape[0]
    n_assign = n_tok * TOP_K_EXPERTS
    onehot = sel_e[:, :, None] == jnp.arange(N_EXPERTS, dtype=I32)[None, None, :]
    per_tok = jnp.sum(onehot.astype(I32), axis=1)
    before = jnp.cumsum(per_tok, axis=0) - per_tok
    counts = before[-1] + per_tok[-1]
    padded = (counts + tmoe - 1) // tmoe * tmoe
    pad_end = jnp.cumsum(padded)
    pad_start = pad_end - padded
    dest = jnp.sum(jnp.where(onehot, (before + pad_start[None, :])[:, None, :], 0), axis=2)
    n_tiles = -(-n_assign // tmoe) + N_EXPERTS
    n_rows = n_tiles * tmoe
    flat_tok = jnp.repeat(jnp.arange(n_tok, dtype=I32), TOP_K_EXPERTS)
    row_tok = jnp.full((n_rows,), n_tok, I32).at[dest.reshape(-1)].set(flat_tok)
    tile_start = jnp.arange(n_tiles, dtype=I32) * tmoe
    tile_e = jnp.sum((tile_start[:, None] >= pad_end[None, :]).astype(I32), axis=1)
    n_used = (pad_end[-1] // tmoe).astype(I32).reshape(1)
    last_e = jnp.max(jnp.where(counts > 0, jnp.arange(N_EXPERTS, dtype=I32), 0))
    tile_e = jnp.where(tile_start < pad_end[-1], jnp.minimum(tile_e, N_EXPERTS - 1), last_e).astype(I32)
    return row_tok, tile_e, n_used, dest


def _pad_cols(a, n):
    return jnp.concatenate([a, jnp.zeros(a.shape[:-1] + (n - a.shape[-1],), a.dtype)], axis=-1)


def _head_pair_slabs(qt, tq):
    d, t = qt.shape
    hpg = MXU_DEPTH // HEAD_DIM
    shape = (N_HEADS, MXU_DEPTH, 1)
    rows = lax.broadcasted_iota(I32, shape, 1) + MXU_DEPTH * (lax.broadcasted_iota(I32, shape, 0) // hpg)
    own = (rows // HEAD_DIM) == lax.broadcasted_iota(I32, shape, 0)
    tiled = jnp.repeat(qt.reshape(d // MXU_DEPTH, MXU_DEPTH, t), hpg, axis=0)
    qz = jnp.where(own, tiled, jnp.zeros((), qt.dtype))
    nb = t // tq
    a = jnp.transpose(qz.reshape(N_HEADS // 2, 2, MXU_DEPTH, nb, tq), (0, 2, 3, 1, 4))
    return a.reshape(N_HEADS // 2, MXU_DEPTH, nb * 2 * tq)


def kernel(x_prompt, x_sample, c_prompt, c_sample, cache_k, cache_v, cache_kidx, page_table, w_ada, b_ada,
           g_pre_mix, w_in, gmlp_ln_g, gmlp_ln_b, gmlp_w_s, gmlp_b_s, w_out, g_post_mix, g_pre_ffn, w_router,
           b_router, w_gate_up, b_gate_up, w_down, b_down, g_post_ffn):
    depth = w_ada.shape[0]
    assert depth == 1 and x_prompt.shape[0] == 1 and x_sample.shape[1] == 1
    _, t, d = x_prompt.shape
    bd = x_sample.shape[0]
    n_pages = page_table.shape[1]
    n_past = n_pages * PAGE_SIZE
    d_attn = N_HEADS * HEAD_DIM
    l = 0
    row2 = lambda a: a.reshape(1, -1)

    c_all = jnp.concatenate([c_prompt, c_sample], axis=0)
    m_pad = -(-c_all.shape[0] // SUBLANES) * SUBLANES
    c_all = jnp.concatenate([c_all, jnp.zeros((m_pad - c_all.shape[0], d), F32)], axis=0)
    mod = _adaln(c_all, w_ada[l], b_ada[l])
    mod_p = [mod[0:1, i * d:(i + 1) * d] for i in range(6)]
    mod_s = [mod[1:1 + bd, i * d:(i + 1) * d] for i in range(6)]

    w_in_l = w_in[l]
    w_in_p = jnp.concatenate([w_in_l[:, :_C_KI], _pad_cols(w_in_l[:, 3072:3136], LANES),
                              _pad_cols(w_in_l[:, 3136:3144], LANES)], axis=1).astype(BF16)
    tril = jnp.tril(jnp.ones((CHUNK, CHUNK), dtype=bool))
    ws_chunk = jnp.where(tril[None], gmlp_w_s[l], 0.0).astype(BF16)
    gw = 512 // GMLP_GROUPS
    bs_chunk = jnp.repeat(jnp.transpose(gmlp_b_s[l]), gw, axis=1)
    ws_row = jnp.repeat(gmlp_w_s[l][:, 0, 0], gw).reshape(1, -1)
    bs_row = jnp.repeat(gmlp_b_s[l][:, 0], gw).reshape(1, -1)
    wo = w_out[l].astype(BF16)
    wo_a, wo_b = wo[:512], wo[512:]
    wr_p = _pad_cols(w_router[l], LANES).astype(BF16)
    br_p = jnp.concatenate([b_router[l], jnp.full((LANES - N_EXPERTS,), NEG_INF, F32)]).reshape(1, LANES)
    g_pre, g_pm, g_pf, g_po = row2(g_pre_mix[l]), row2(g_post_mix[l]), row2(g_pre_ffn[l]), row2(g_post_ffn[l])
    ln_g, ln_b = row2(gmlp_ln_g[l]), row2(gmlp_ln_b[l])

    xp = x_prompt[0]
    tq = 128
    (q_p, k_p, kb_p, v_p, vb_p, qi_p, ki_p, kib_p, wi_p, oa_p, _) = _premix(
        xp, mod_p[0], mod_p[1], g_pre, w_in_p, ln_g, ln_b, ws_chunk, bs_chunk, tm=256, chunked=True)
    ob_p = _dsa_prompt(_head_pair_slabs(jnp.transpose(q_p), tq), jnp.transpose(qi_p), jnp.transpose(wi_p),
                       kib_p, kb_p, jnp.transpose(vb_p), tq=tq, tk=min(512, t), tk3=min(512, t))
    x1_p, h2_p, se_p, sw_p = _postmix(oa_p, ob_p, wo_a, wo_b, xp, mod_p[2], mod_p[3], mod_p[4], g_pm, g_pf,
                                      wr_p, br_p, tm=256)

    xs_ = x_sample[:, 0]
    (q_s, k_s, _, v_s, _, qi_s, ki_s, _, wi_s, oa_s, vn_s) = _premix(
        xs_, mod_s[0], mod_s[1], g_pre, w_in_p, ln_g, ln_b, ws_row, bs_row, tm=bd, chunked=False)
    page_flat = page_table.reshape(-1)
    pg = 16 if n_pages % 16 == 0 else 1
    scores, snew = _idx_sample(page_flat, qi_s.reshape(bd, IDX_HEADS, IDX_DIM), wi_s.reshape(bd, IDX_HEADS, 1),
                               ki_s.reshape(bd, 1, IDX_DIM), cache_kidx, l, n_pages=n_pages, pg=pg)
    topk_s = min(TOPK_MAX, (n_past + 1) // 4)
    sel = _topk_sample(scores, snew, topk_s)[:, :, 0]
    is_new = sel >= n_past
    newpos = jnp.where(jnp.any(is_new, axis=1), jnp.argmax(is_new, axis=1), -1).astype(I32)
    sp = jnp.minimum(sel, n_past - 1)
    phys = jnp.take_along_axis(page_table, sp // PAGE_SIZE, axis=1).astype(I32).reshape(-1)
    row = (sp % PAGE_SIZE).astype(I32).reshape(-1)
    hd = (bd, N_HEADS, HEAD_DIM)
    ob_s = _attn_gather(phys, row, newpos, q_s.reshape(hd), k_s.reshape(hd), v_s.reshape(hd),
                        cache_k, cache_v, l, topk=topk_s)
    x1_s, h2_s, se_s, sw_s = _postmix(oa_s, ob_s.reshape(bd, d_attn).astype(BF16), wo_a, wo_b, xs_, mod_s[2],
                                      mod_s[3], mod_s[4], g_pm, g_pf, wr_p, br_p, tm=bd)

    tmoe = 256
    h2_all = jnp.concatenate([h2_p, h2_s, jnp.zeros((1, d), F32)], axis=0)
    sel_e = jnp.concatenate([se_p[:, :TOP_K_EXPERTS], se_s[:, :TOP_K_EXPERTS]], axis=0)
    row_tok, tile_e, n_used, dest = _route(sel_e, tmoe)
    ys = _moe(tile_e, n_used, h2_all[row_tok], w_gate_up[l], b_gate_up[l], w_down[l], b_down[l], tmoe=tmoe)
    y_p = _final(ys[jnp.transpose(dest[:t])], sw_p, x1_p, mod_p[5], g_po, tm=256)
    y_s = _final(ys[jnp.transpose(dest[t:])], sw_s, x1_s, mod_s[5], g_po, tm=bd)

    hs = (N_HEADS, HEAD_DIM)
    return (y_p[None], y_s[:, None],
            k_p.reshape(1, 1, t, *hs), v_p.reshape(1, 1, t, *hs), ki_p.reshape(1, 1, t, IDX_DIM),
            k_s.reshape(1, bd, 1, *hs), v_s.reshape(1, bd, 1, *hs), ki_s.reshape(1, bd, 1, IDX_DIM),
            vn_s.reshape(1, bd, 1, -1))
```

```python
import functools
import math

import jax
import jax.numpy as jnp
from jax import lax
from jax.experimental import pallas as pl
from jax.experimental.pallas import tpu as pltpu

F32 = jnp.float32
BF16 = jnp.bfloat16
I32 = jnp.int32

EPS = 1e-6
N_HEADS = 8
HEAD_DIM = 64
IDX_HEADS = 8
IDX_DIM = 64
GMLP_GROUPS = 4
CHUNK = 128
TOPK_MAX = 256
PAGE_SIZE = 128
N_EXPERTS = 32
TOP_K_EXPERTS = 4
SWIGLU_LIMIT = 7.0
SWIGLU_ALPHA = 1.702
LOG2E = 1.4426950408889634

LANES = 128
SUBLANES = 8
MXU_DEPTH = 256
VMEM_LIMIT = 60000 * 1024
INT_MIN = -2147483648
INT_MAX = 2147483647
NEG_INF = float("-inf")


def _cparams(*sem):
    return pltpu.CompilerParams(dimension_semantics=sem, vmem_limit_bytes=VMEM_LIMIT)


def _gelu_tanh(x):
    return 0.5 * x * (1.0 + jnp.tanh(0.7978845608028654 * (x + 0.044715 * (x * x * x))))


def _rms(x, g):
    return x * lax.rsqrt(jnp.mean(x * x, axis=-1, keepdims=True) + EPS) * g


def _ordered_bits_to_float(u):
    bits = jnp.where(u >= 0, u, u ^ jnp.int32(0x7FFFFFFF))
    return lax.bitcast_convert_type(bits, F32)


def _adaln_kernel(c_ref, w_ref, b_ref, o_ref):
    c = c_ref[...]
    a = c * (1.0 / (1.0 + jnp.exp(-c)))
    o_ref[...] = jnp.dot(a.astype(BF16), w_ref[...].astype(BF16), preferred_element_type=F32) + b_ref[...]


def _adaln(c_all, w_ada, b_ada):
    m, d = c_all.shape
    n = w_ada.shape[1]
    tn = 1536
    return pl.pallas_call(
        _adaln_kernel,
        out_shape=jax.ShapeDtypeStruct((m, n), F32),
        grid=(n // tn,),
        in_specs=[pl.BlockSpec((m, d), lambda j: (0, 0)),
                  pl.BlockSpec((d, tn), lambda j: (0, j)),
                  pl.BlockSpec((1, tn), lambda j: (0, j))],
        out_specs=pl.BlockSpec((m, tn), lambda j: (0, j)),
        compiler_params=_cparams("arbitrary"),
        name="adaln",
    )(c_all, w_ada, b_ada.reshape(1, n))


_C_U, _C_V, _C_Q, _C_K, _C_VV, _C_QI, _C_KI, _C_WI, _C_END = 0, 512, 1024, 1536, 2048, 2560, 3072, 3200, 3328


def _premix_kernel(x_ref, sh_ref, sc_ref, g_ref, w_ref, lg_ref, lb_ref, ws_ref, bs_ref,
                   q_ref, k_ref, kb_ref, v_ref, vb_ref, qi_ref, ki_ref, kib_ref, wi_ref, oa_ref, vn_ref,
                   *, tm, chunked):
    x = x_ref[...]
    h = _rms(x, g_ref[...]) * (1.0 + sc_ref[...]) + sh_ref[...]
    p = jnp.dot(h.astype(BF16), w_ref[...], preferred_element_type=F32)
    k = p[:, _C_K:_C_VV]
    vv = p[:, _C_VV:_C_QI]
    ki = p[:, _C_KI:_C_KI + IDX_DIM]
    q_ref[...] = (p[:, _C_Q:_C_K] * (HEAD_DIM ** -0.5 * LOG2E)).astype(BF16)
    k_ref[...] = k
    kb_ref[...] = k.astype(BF16)
    v_ref[...] = vv
    vb_ref[...] = vv.astype(BF16)
    qi_ref[...] = (p[:, _C_QI:_C_KI] * (IDX_DIM ** -0.5)).astype(BF16)
    ki_ref[...] = ki
    kib_ref[...] = ki.astype(BF16)
    wi_ref[...] = p[:, _C_WI:_C_WI + IDX_HEADS] * (IDX_HEADS ** -0.5)
    gu = _gelu_tanh(p[:, _C_U:_C_V])
    gv = _gelu_tanh(p[:, _C_V:_C_Q])
    mu = jnp.mean(gv, axis=-1, keepdims=True)
    dv = gv - mu
    var = jnp.mean(dv * dv, axis=-1, keepdims=True)
    vn = dv * lax.rsqrt(var + EPS) * lg_ref[...] + lb_ref[...]
    vn_ref[...] = vn
    if chunked:
        gw = vn.shape[1] // GMLP_GROUPS
        vnb = vn.astype(BF16)
        for r in range(tm // CHUNK):
            rs = slice(r * CHUNK, (r + 1) * CHUNK)
            for g in range(GMLP_GROUPS):
                cs = slice(g * gw, (g + 1) * gw)
                s = jnp.dot(ws_ref[g], vnb[rs, cs], preferred_element_type=F32) + bs_ref[:, cs]
                oa_ref[rs, cs] = (gu[rs, cs] * s).astype(BF16)
    else:
        oa_ref[...] = (gu * (vn * ws_ref[...] + bs_ref[...])).astype(BF16)


def _premix(x, shift, scale, g_pre, w_in_p, ln_g, ln_b, ws, bs, *, tm, chunked):
    t, d = x.shape
    per_row = shift.shape[0] != 1
    mod_spec = pl.BlockSpec((tm, d), lambda i: (i, 0)) if per_row else pl.BlockSpec((1, d), lambda i: (0, 0))
    const2 = lambda a: pl.BlockSpec(a.shape, lambda i: (0, 0))
    ws_spec = pl.BlockSpec(ws.shape, lambda i: (0, 0, 0)) if chunked else const2(ws)
    row = lambda n: pl.BlockSpec((tm, n), lambda i: (i, 0))
    outs = [((t, 512), BF16), ((t, 512), F32), ((t, 512), BF16), ((t, 512), F32), ((t, 512), BF16),
            ((t, 512), BF16), ((t, IDX_DIM), F32), ((t, IDX_DIM), BF16), ((t, IDX_HEADS), F32),
            ((t, 512), BF16), ((t, 512), F32)]
    return pl.pallas_call(
        functools.partial(_premix_kernel, tm=tm, chunked=chunked),
        out_shape=[jax.ShapeDtypeStruct(s, dt) for s, dt in outs],
        grid=(t // tm,),
        in_specs=[row(d), mod_spec, mod_spec, const2(g_pre), const2(w_in_p), const2(ln_g), const2(ln_b),
                  ws_spec, const2(bs)],
        out_specs=[row(s[1]) for s, _ in outs],
        compiler_params=_cparams("arbitrary"),
        name="premix_chunked" if chunked else "premix_rows",
    )(x, shift, scale, g_pre, w_in_p, ln_g, ln_b, ws, bs)


_CNT_ROWS = 64
_RED_ROWS = 32


def _dsa_prompt_kernel(qz2_ref, qit_ref, wt_ref, kidx_ref, k_ref, vt_ref, o_ref,
                       s_ref, m_ref, l_ref, acc_ref, lg_ref, p_ref, *, tq, tk, tk3, topk, nbits):
    q0 = pl.program_id(0) * tq
    n_chunks = (q0 + tq + tk - 1) // tk
    n_chunks3 = (q0 + tq + tk3 - 1) // tk3
    qpos = q0 + lax.broadcasted_iota(I32, (1, tq), 1)
    w = wt_ref[...]

    def key_pos(c, size):
        return c * size + lax.broadcasted_iota(I32, (size, 1), 0)

    def p1(c, carry):
        k0 = pl.multiple_of(c * tk, tk)
        kc = kidx_ref[pl.ds(k0, tk), :]
        acc = jnp.zeros((tk, tq), F32)
        for h in range(IDX_HEADS):
            d = jnp.dot(kc, qit_ref[h * IDX_DIM:(h + 1) * IDX_DIM, :], preferred_element_type=F32)
            acc = acc + w[h:h + 1, :] * jnp.maximum(d, 0.0)
        s_ref[pl.ds(k0, tk), :] = jnp.where(key_pos(c, tk) <= qpos, acc, NEG_INF)
        return carry

    lax.fori_loop(0, n_chunks, p1, 0)

    def count(pred):
        def body(c, acc):
            k0 = pl.multiple_of(c * tk, tk)
            hit = jnp.where(pred(s_ref[pl.ds(k0, tk), :], c), 1.0, 0.0)
            return acc + jnp.sum(hit.reshape(tk // _CNT_ROWS, _CNT_ROWS, tq), axis=0)
        acc = lax.fori_loop(0, n_chunks, body, jnp.zeros((_CNT_ROWS, tq), F32))
        return jnp.sum(acc, axis=0, keepdims=True)

    kf = float(topk)
    select_all = (qpos + 1) <= topk
    cnt0 = count(lambda s, c: s >= 0.0)
    nonneg = cnt0 >= kf
    base0 = jnp.where(nonneg, jnp.int32(0), jnp.int32(INT_MIN))
    cntb0 = jnp.where(nonneg, cnt0, (qpos + 1).astype(F32))

    def bisect(b, state):
        base, cntb = state
        cand = base | jnp.left_shift(jnp.int32(1), 30 - b)
        thr = _ordered_bits_to_float(cand)
        c = count(lambda s, cc: s >= thr)
        ok = c >= kf
        return jnp.where(ok, cand, base), jnp.where(ok, c, cntb)

    base, cntb = lax.fori_loop(0, 31, bisect, (base0, cntb0))
    tau = jnp.where(select_all, NEG_INF, _ordered_bits_to_float(base))
    straddle = jnp.logical_and(cntb > kf, jnp.logical_not(select_all))

    def tie_search():
        need = kf - count(lambda s, c: s > tau)

        def step(b, j):
            cand = j | jnp.left_shift(jnp.int32(1), nbits - 1 - b)
            below = count(lambda s, c: jnp.logical_and(s == tau, key_pos(c, tk) < cand))
            return jnp.where(below < need, cand, j)
        return lax.fori_loop(0, nbits, step, jnp.zeros((1, tq), I32))

    any_straddle = jnp.max(jnp.where(straddle, 1, 0)) > 0
    j_tie = lax.cond(any_straddle, tie_search, lambda: jnp.zeros((1, tq), I32))
    j_tie = jnp.where(straddle, j_tie, jnp.int32(INT_MAX))

    m_ref[...] = jnp.full(m_ref.shape, NEG_INF, F32)
    l_ref[...] = jnp.zeros(l_ref.shape, F32)
    acc_ref[...] = jnp.zeros(acc_ref.shape, F32)
    hpg = MXU_DEPTH // HEAD_DIM
    col = lambda x, op: op(op(x.reshape(tk3 // _RED_ROWS, _RED_ROWS, tq), axis=0), axis=0, keepdims=True)

    def chunk_start(c):
        return pl.multiple_of(jnp.minimum(c, n_chunks3 - 1) * tk3, tk3)

    def logits(c, buf):
        k0 = chunk_start(c)
        s = s_ref[pl.ds(k0, tk3), :]
        kp = k0 + lax.broadcasted_iota(I32, (tk3, 1), 0)
        sel = jnp.logical_or(s > tau, jnp.logical_and(s == tau, kp <= j_tie))
        live = jnp.logical_and(kp <= qpos, c < n_chunks3)
        bias = jnp.where(jnp.logical_and(sel, live), 0.0, NEG_INF)
        cmax = []
        for pr in range(N_HEADS // 2):
            g = (2 * pr) // hpg
            kc = k_ref[pl.ds(k0, tk3), g * MXU_DEPTH:(g + 1) * MXU_DEPTH]
            lg2 = jnp.dot(kc, qz2_ref[pr], preferred_element_type=F32)
            for half in range(2):
                lg = lg2[:, half * tq:(half + 1) * tq] + bias
                lg_ref[buf, 2 * pr + half] = lg
                cmax.append(col(lg, jnp.max))
        return jnp.concatenate(cmax, axis=0)

    def softmax_pv(c, buf, cmax):
        k0 = chunk_start(c)
        m_old = m_ref[...]
        m_new = jnp.maximum(m_old, cmax)
        m_safe = jnp.where(m_new == NEG_INF, 0.0, m_new)
        alpha = jnp.exp2(m_old - m_safe)
        m_ref[...] = m_new
        psum = []
        for h in range(N_HEADS):
            p = jnp.exp2(lg_ref[buf, h] - m_safe[h:h + 1, :])
            p_ref[buf, h] = p.astype(BF16)
            psum.append(col(p, jnp.sum))
        l_ref[...] = alpha * l_ref[...] + jnp.concatenate(psum, axis=0)
        for h in range(N_HEADS):
            hs = slice(h * HEAD_DIM, (h + 1) * HEAD_DIM)
            pv = jnp.dot(vt_ref[hs, pl.ds(k0, tk3)], p_ref[buf, h], preferred_element_type=F32)
            acc_ref[hs, :] = alpha[h:h + 1, :] * acc_ref[hs, :] + pv

    def p3(j, cmax_even):
        cmax_odd = logits(2 * j + 1, 1)
        softmax_pv(2 * j, 0, cmax_even)
        cmax_even = logits(2 * j + 2, 0)
        softmax_pv(2 * j + 1, 1, cmax_odd)
        return cmax_even

    lax.fori_loop(0, (n_chunks3 + 1) // 2, p3, logits(0, 0))
    for h in range(N_HEADS):
        hs = slice(h * HEAD_DIM, (h + 1) * HEAD_DIM)
        acc_ref[hs, :] = acc_ref[hs, :] * (1.0 / l_ref[h:h + 1, :])
    o_ref[...] = jnp.transpose(acc_ref[...]).astype(BF16)


def _dsa_prompt(qz2, qit, wt, kidx_b, k_b, vt_b, *, tq, tk, tk3):
    t = k_b.shape[0]
    topk = min(TOPK_MAX, t // 4)
    nbits = max(1, math.ceil(math.log2(t)))
    resident = lambda a: pl.BlockSpec(a.shape, lambda i: (0,) * a.ndim, pipeline_mode=pl.Buffered(1))
    return pl.pallas_call(
        functools.partial(_dsa_prompt_kernel, tq=tq, tk=tk, tk3=tk3, topk=topk, nbits=nbits),
        out_shape=jax.ShapeDtypeStruct((t, N_HEADS * HEAD_DIM), BF16),
        grid=(t // tq,),
        in_specs=[pl.BlockSpec((N_HEADS // 2, MXU_DEPTH, 2 * tq), lambda i: (0, 0, i)),
                  pl.BlockSpec((IDX_HEADS * IDX_DIM, tq), lambda i: (0, i)),
                  pl.BlockSpec((IDX_HEADS, tq), lambda i: (0, i)),
                  resident(kidx_b), resident(k_b), resident(vt_b)],
        out_specs=pl.BlockSpec((tq, N_HEADS * HEAD_DIM), lambda i: (i, 0)),
        scratch_shapes=[pltpu.VMEM((t, tq), F32), pltpu.VMEM((N_HEADS, tq), F32),
                        pltpu.VMEM((N_HEADS, tq), F32), pltpu.VMEM((N_HEADS * HEAD_DIM, tq), F32),
                        pltpu.VMEM((2, N_HEADS, tk3, tq), F32), pltpu.VMEM((2, N_HEADS, tk3, tq), BF16)],
        compiler_params=_cparams("arbitrary"),
        name="dsa_prompt",
    )(qz2, qit, wt, kidx_b, k_b, vt_b)


def _idx_sample_kernel(pt_ref, q_ref, w_ref, kn_ref, *rest, pg):
    page_refs, s_ref, sn_ref = rest[:pg], rest[pg], rest[pg + 1]
    q = q_ref[...]
    w = w_ref[...]
    for r in range(pg):
        kp = page_refs[r][...].astype(BF16)
        d = jnp.dot(q, kp, preferred_element_type=F32)
        s_ref[r:r + 1, :] = jnp.sum(w * jnp.maximum(d, 0.0), axis=0, keepdims=True)
    kn = kn_ref[...].astype(BF16).astype(F32)
    dn = jnp.sum(q.astype(F32) * kn, axis=1, keepdims=True)
    sn = jnp.sum(w * jnp.maximum(dn, 0.0), axis=0, keepdims=True)
    sn_ref[...] = jnp.broadcast_to(sn, sn_ref.shape)


def _idx_sample(page_flat, qi_s, w_s, kidx_new, cache_kidx, layer, *, n_pages, pg):
    bd = qi_s.shape[0]
    page_spec = lambda r: pl.BlockSpec((None, None, IDX_DIM, PAGE_SIZE),
                                       lambda b, j, pt: (layer, pt[b * n_pages + j * pg + r], 0, 0))
    grid_spec = pltpu.PrefetchScalarGridSpec(
        num_scalar_prefetch=1,
        grid=(bd, n_pages // pg),
        in_specs=[pl.BlockSpec((None, IDX_HEADS, IDX_DIM), lambda b, j, pt: (b, 0, 0)),
                  pl.BlockSpec((None, IDX_HEADS, 1), lambda b, j, pt: (b, 0, 0)),
                  pl.BlockSpec((None, 1, IDX_DIM), lambda b, j, pt: (b, 0, 0))]
                 + [page_spec(r) for r in range(pg)],
        out_specs=[pl.BlockSpec((None, pg, PAGE_SIZE), lambda b, j, pt: (b, j, 0)),
                   pl.BlockSpec((None, 1, LANES), lambda b, j, pt: (b, 0, 0))],
    )
    return pl.pallas_call(
        functools.partial(_idx_sample_kernel, pg=pg),
        out_shape=[jax.ShapeDtypeStruct((bd, n_pages, PAGE_SIZE), F32),
                   jax.ShapeDtypeStruct((bd, 1, LANES), F32)],
        grid_spec=grid_spec,
        compiler_params=_cparams("arbitrary", "arbitrary"),
        name="idx_sample",
    )(page_flat, qi_s, w_s, kidx_new, *([cache_kidx] * pg))


def _topk_sample_kernel(s_ref, sn_ref, mask_ref, mnew_ref, *, topk, nbits):
    s = s_ref[...]
    sn = sn_ref[...][:, 0:1]
    n_pg, width = s.shape
    n_past = n_pg * width
    kpos = lax.broadcasted_iota(I32, s.shape, 0) * width + lax.broadcasted_iota(I32, s.shape, 1)
    kf = float(topk)

    def count(pred_past, pred_new):
        c = jnp.sum(jnp.sum(jnp.where(pred_past, 1.0, 0.0), axis=0, keepdims=True), axis=1, keepdims=True)
        return c + jnp.where(pred_new, 1.0, 0.0)

    cnt0 = count(s >= 0.0, sn >= 0.0)
    base0 = jnp.where(cnt0 >= kf, jnp.int32(0), jnp.int32(INT_MIN))

    def bisect(b, base):
        cand = base | jnp.left_shift(jnp.int32(1), 30 - b)
        thr = _ordered_bits_to_float(cand)
        return jnp.where(count(s >= thr, sn >= thr) >= kf, cand, base)

    tau = _ordered_bits_to_float(lax.fori_loop(0, 31, bisect, base0))
    need = kf - count(s > tau, sn > tau)

    def step(b, j):
        cand = j | jnp.left_shift(jnp.int32(1), nbits - 1 - b)
        below = count(jnp.logical_and(s == tau, kpos < cand), jnp.logical_and(sn == tau, n_past < cand))
        return jnp.where(below < need, cand, j)

    j_tie = lax.fori_loop(0, nbits, step, jnp.zeros((1, 1), I32))
    sel = jnp.logical_or(s > tau, jnp.logical_and(s == tau, kpos <= j_tie))
    sel_new = jnp.logical_or(sn > tau, jnp.logical_and(sn == tau, n_past <= j_tie))
    mask_ref[...] = jnp.where(sel, 1.0, 0.0)
    mnew_ref[...] = jnp.broadcast_to(jnp.where(sel_new, 1.0, 0.0), mnew_ref.shape)


def _topk_sample(scores, snew, topk):
    bd, n_pages, _ = scores.shape
    nbits = max(1, math.ceil(math.log2(n_pages * PAGE_SIZE + 1)))
    return pl.pallas_call(
        functools.partial(_topk_sample_kernel, topk=topk, nbits=nbits),
        out_shape=[jax.ShapeDtypeStruct(scores.shape, F32), jax.ShapeDtypeStruct(snew.shape, F32)],
        grid=(bd,),
        in_specs=[pl.BlockSpec((None, n_pages, PAGE_SIZE), lambda b: (b, 0, 0)),
                  pl.BlockSpec((None, 1, LANES), lambda b: (b, 0, 0))],
        out_specs=[pl.BlockSpec((None, n_pages, PAGE_SIZE), lambda b: (b, 0, 0)),
                   pl.BlockSpec((None, 1, LANES), lambda b: (b, 0, 0))],
        compiler_params=_cparams("arbitrary"),
        name="topk_sample",
    )(scores, snew)


def _attn_sample_kernel(pt_ref, qz_ref, kn_ref, vn_ref, mask_ref, mnew_ref, *rest, pg):
    k_refs, v_refs = rest[:pg], rest[pg:2 * pg]
    o_ref, m_ref, l_ref, acc_ref = rest[2 * pg:]
    j = pl.program_id(1)
    qz = qz_ref[...]
    d_attn = qz.shape[1]

    @pl.when(j == 0)
    def _():
        sel_new = mnew_ref[...][:, 0:1] > 0.0
        kn = kn_ref[...].astype(BF16).astype(F32)
        lg = jnp.sum(qz.astype(F32) * kn, axis=1, keepdims=True)
        m_ref[...] = jnp.where(sel_new, lg, NEG_INF)
        l_ref[...] = jnp.where(sel_new, jnp.ones_like(lg), 0.0)
        vn = vn_ref[...].astype(BF16).astype(F32)
        acc_ref[...] = jnp.where(sel_new, jnp.broadcast_to(vn, acc_ref.shape), 0.0)

    lgs = []
    for r in range(pg):
        lg = jnp.dot(qz, k_refs[r][...].astype(BF16), preferred_element_type=F32)
        lgs.append(jnp.where(mask_ref[r:r + 1, :] > 0.0, lg, NEG_INF))
    lg_max = lgs[0]
    for r in range(1, pg):
        lg_max = jnp.maximum(lg_max, lgs[r])
    m_old = m_ref[...]
    m_new = jnp.maximum(m_old, jnp.max(lg_max, axis=1, keepdims=True))
    m_safe = jnp.where(m_new == NEG_INF, 0.0, m_new)
    alpha = jnp.exp2(m_old - m_safe)
    p_sum = None
    pv = None
    for r in range(pg):
        p = jnp.exp2(lgs[r] - m_safe)
        p_sum = p if r == 0 else p_sum + p
        pv_r = lax.dot_general(p.astype(BF16), v_refs[r][...].astype(BF16), (((1,), (1,)), ((), ())),
                               preferred_element_type=F32)
        pv = pv_r if r == 0 else pv + pv_r
    l_ref[...] = alpha * l_ref[...] + jnp.sum(p_sum, axis=1, keepdims=True)
    acc_ref[...] = alpha * acc_ref[...] + pv
    m_ref[...] = m_new

    @pl.when(j == pl.num_programs(1) - 1)
    def _():
        head_of_lane = lax.broadcasted_iota(I32, (N_HEADS, d_attn), 1) // HEAD_DIM
        own = head_of_lane == lax.broadcasted_iota(I32, (N_HEADS, d_attn), 0)
        o = jnp.where(own, acc_ref[...] * (1.0 / l_ref[...]), 0.0)
        o_ref[...] = jnp.sum(o, axis=0, keepdims=True)


def _attn_sample(page_flat, qz_s, k_new, v_new, mask, mnew, cache_kt, cache_vt, layer, *, n_pages, pg):
    bd, _, d_attn = qz_s.shape
    page_spec = lambda r: pl.BlockSpec((None, None, d_attn, PAGE_SIZE),
                                       lambda b, j, pt: (layer, pt[b * n_pages + j * pg + r], 0, 0))
    per_b = lambda n, w: pl.BlockSpec((None, n, w), lambda b, j, pt: (b, 0, 0))
    grid_spec = pltpu.PrefetchScalarGridSpec(
        num_scalar_prefetch=1,
        grid=(bd, n_pages // pg),
        in_specs=[per_b(N_HEADS, d_attn), per_b(1, d_attn), per_b(1, d_attn),
                  pl.BlockSpec((None, pg, PAGE_SIZE), lambda b, j, pt: (b, j, 0)),
                  per_b(1, LANES)]
                 + [page_spec(r) for r in range(pg)] * 2,
        out_specs=per_b(1, d_attn),
        scratch_shapes=[pltpu.VMEM((N_HEADS, 1), F32), pltpu.VMEM((N_HEADS, 1), F32),
                        pltpu.VMEM((N_HEADS, d_attn), F32)],
    )
    return pl.pallas_call(
        functools.partial(_attn_sample_kernel, pg=pg),
        out_shape=jax.ShapeDtypeStruct((bd, 1, d_attn), F32),
        grid_spec=grid_spec,
        compiler_params=_cparams("arbitrary", "arbitrary"),
        name="attn_sample",
    )(page_flat, qz_s, k_new, v_new, mask, mnew, *([cache_kt] * pg), *([cache_vt] * pg))


def _postmix_kernel(oa_ref, ob_ref, woa_ref, wob_ref, x_ref, g1_ref, sh_ref, sc_ref, gpm_ref, gpf_ref,
                    wr_ref, br_ref, x1_ref, h2_ref, se_ref, sw_ref):
    mix = (jnp.dot(oa_ref[...], woa_ref[...], preferred_element_type=F32)
           + jnp.dot(ob_ref[...], wob_ref[...], preferred_element_type=F32))
    x1 = x_ref[...] + g1_ref[...] * _rms(mix, gpm_ref[...])
    x1_ref[...] = x1
    h2 = _rms(x1, gpf_ref[...]) * (1.0 + sc_ref[...]) + sh_ref[...]
    h2b = h2.astype(BF16)
    h2_ref[...] = h2
    logits = jnp.dot(h2b, wr_ref[...], preferred_element_type=F32) + br_ref[...]
    lane = lax.broadcasted_iota(I32, logits.shape, 1)
    lane_f = lane.astype(F32)
    se = jnp.zeros(logits.shape, F32)
    sw = jnp.zeros(logits.shape, F32)
    top = None
    denom = None
    for r in range(TOP_K_EXPERTS):
        m = jnp.max(logits, axis=1, keepdims=True)
        idx = jnp.min(jnp.where(logits == m, lane_f, float(LANES)), axis=1, keepdims=True)
        if r == 0:
            top = m
        e = jnp.exp(m - top)
        denom = e if r == 0 else denom + e
        se = jnp.where(lane == r, idx, se)
        sw = jnp.where(lane == r, e, sw)
        logits = jnp.where(lane_f == idx, NEG_INF, logits)
    se_ref[...] = se.astype(I32)
    sw_ref[...] = sw * (1.0 / denom)


def _postmix(out_a, out_b, wo_a, wo_b, x, gate1, shift2, scale2, g_pm, g_pf, wr_p, br_p, *, tm):
    t, d = x.shape
    per_row = gate1.shape[0] != 1
    mod_spec = pl.BlockSpec((tm, d), lambda i: (i, 0)) if per_row else pl.BlockSpec((1, d), lambda i: (0, 0))
    const2 = lambda a: pl.BlockSpec(a.shape, lambda i: (0, 0))
    row = lambda n: pl.BlockSpec((tm, n), lambda i: (i, 0))
    return pl.pallas_call(
        _postmix_kernel,
        out_shape=[jax.ShapeDtypeStruct((t, d), F32), jax.ShapeDtypeStruct((t, d), F32),
                   jax.ShapeDtypeStruct((t, LANES), I32), jax.ShapeDtypeStruct((t, LANES), F32)],
        grid=(t // tm,),
        in_specs=[row(out_a.shape[1]), row(out_b.shape[1]), const2(wo_a), const2(wo_b), row(d),
                  mod_spec, mod_spec, mod_spec, const2(g_pm), const2(g_pf), const2(wr_p), const2(br_p)],
        out_specs=[row(d), row(d), row(LANES), row(LANES)],
        compiler_params=_cparams("arbitrary"),
        name="postmix",
    )(out_a, out_b, wo_a, wo_b, x, gate1, shift2, scale2, g_pm, g_pf, wr_p, br_p)


def _moe_kernel(te_ref, nu_ref, x_ref, wgu_ref, bgu_ref, wd_ref, bd_ref, o_ref, wgu_b, wd_b, *, d_ff):
    i = pl.program_id(0)
    changed = jnp.logical_or(i == 0, te_ref[i] != te_ref[jnp.maximum(i - 1, 0)])

    @pl.when(changed)
    def _():
        rows = 128
        def cast(r, carry):
            r0 = pl.multiple_of(r * rows, rows)
            wgu_b[pl.ds(r0, rows), :] = wgu_ref[pl.ds(r0, rows), :].astype(BF16)
            wd_b[pl.ds(r0, rows), :] = wd_ref[pl.ds(r0, rows), :].astype(BF16)
            return carry
        lax.fori_loop(0, wgu_b.shape[0] // rows, cast, 0)

    @pl.when(i < nu_ref[0])
    def _():
        gu = jnp.dot(x_ref[...].astype(BF16), wgu_b[...], preferred_element_type=F32) + bgu_ref[...]
        gate = jnp.minimum(gu[:, :d_ff], SWIGLU_LIMIT)
        up = jnp.clip(gu[:, d_ff:], -SWIGLU_LIMIT, SWIGLU_LIMIT)
        a = (up + 1.0) * (gate * (1.0 / (1.0 + jnp.exp(-SWIGLU_ALPHA * gate))))
        o_ref[...] = jnp.dot(a.astype(BF16), wd_b[...], preferred_element_type=F32) + bd_ref[...]

    @pl.when(i >= nu_ref[0])
    def _():
        o_ref[...] = jnp.zeros(o_ref.shape, F32)


def _moe(tile_e, n_used, xs, w_gate_up, b_gate_up, w_down, b_down, *, tmoe):
    n_rows, d = xs.shape
    n_tiles = n_rows // tmoe
    d_ff = w_down.shape[1]
    assert w_gate_up.shape[1] == d and w_down.shape[1] == w_down.shape[2] == d
    grid_spec = pltpu.PrefetchScalarGridSpec(
        num_scalar_prefetch=2,
        grid=(n_tiles,),
        in_specs=[pl.BlockSpec((tmoe, d), lambda i, te, nu: (i, 0)),
                  pl.BlockSpec((None, d, 2 * d_ff), lambda i, te, nu: (te[i], 0, 0)),
                  pl.BlockSpec((None, 1, 2 * d_ff), lambda i, te, nu: (te[i], 0, 0)),
                  pl.BlockSpec((None, d_ff, d), lambda i, te, nu: (te[i], 0, 0)),
                  pl.BlockSpec((None, 1, d), lambda i, te, nu: (te[i], 0, 0))],
        out_specs=pl.BlockSpec((tmoe, d), lambda i, te, nu: (i, 0)),
        scratch_shapes=[pltpu.VMEM((d, 2 * d_ff), BF16), pltpu.VMEM((d_ff, d), BF16)],
    )
    return pl.pallas_call(
        functools.partial(_moe_kernel, d_ff=d_ff),
        out_shape=jax.ShapeDtypeStruct((n_rows, d), F32),
        grid_spec=grid_spec,
        compiler_params=_cparams("arbitrary"),
        name="moe",
    )(tile_e, n_used, xs, w_gate_up, b_gate_up.reshape(N_EXPERTS, 1, -1), w_down,
      b_down.reshape(N_EXPERTS, 1, -1))


def _final_kernel(y4_ref, sw_ref, x1_ref, g2_ref, gpf_ref, o_ref):
    sw = sw_ref[...]
    f = ((y4_ref[0] * sw[:, 0:1] + y4_ref[1] * sw[:, 1:2]) + (y4_ref[2] * sw[:, 2:3] + y4_ref[3] * sw[:, 3:4]))
    o_ref[...] = x1_ref[...] + g2_ref[...] * _rms(f, gpf_ref[...])


def _final(y4, sw, x1, gate2, g_post_ffn, *, tm):
    t, d = x1.shape
    per_row = gate2.shape[0] != 1
    mod_spec = pl.BlockSpec((tm, d), lambda i: (i, 0)) if per_row else pl.BlockSpec((1, d), lambda i: (0, 0))
    return pl.pallas_call(
        _final_kernel,
        out_shape=jax.ShapeDtypeStruct((t, d), F32),
        grid=(t // tm,),
        in_specs=[pl.BlockSpec((TOP_K_EXPERTS, tm, d), lambda i: (0, i, 0)),
                  pl.BlockSpec((tm, LANES), lambda i: (i, 0)),
                  pl.BlockSpec((tm, d), lambda i: (i, 0)), mod_spec,
                  pl.BlockSpec((1, d), lambda i: (0, 0))],
        out_specs=pl.BlockSpec((tm, d), lambda i: (i, 0)),
        compiler_params=_cparams("arbitrary"),
        name="final",
    )(y4, sw, x1, gate2, g_post_ffn)


def _route(sel_e, tmoe):
    n_tok = sel_e.shape[0]
    n_assign = n_tok * TOP_K_EXPERTS
    onehot = sel_e[:, :, None] == jnp.arange(N_EXPERTS, dtype=I32)[None, None, :]
    per_tok = jnp.sum(onehot.astype(I32), axis=1)
    before = jnp.cumsum(per_tok, axis=0) - per_tok
    counts = before[-1] + per_tok[-1]
    padded = (counts + tmoe - 1) // tmoe * tmoe
    pad_end = jnp.cumsum(padded)
    pad_start = pad_end - padded
    dest = jnp.sum(jnp.where(onehot, (before + pad_start[None, :])[:, None, :], 0), axis=2)
    n_tiles = -(-n_assign // tmoe) + N_EXPERTS
    n_rows = n_tiles * tmoe
    flat_tok = jnp.repeat(jnp.arange(n_tok, dtype=I32), TOP_K_EXPERTS)
    row_tok = jnp.full((n_rows,), n_tok, I32).at[dest.reshape(-1)].set(flat_tok)
    tile_start = jnp.arange(n_tiles, dtype=I32) * tmoe
    tile_e = jnp.sum((tile_start[:, None] >= pad_end[None, :]).astype(I32), axis=1)
    n_used = (pad_end[-1] // tmoe).astype(I32).reshape(1)
    last_e = jnp.max(jnp.where(counts > 0, jnp.arange(N_EXPERTS, dtype=I32), 0))
    tile_e = jnp.where(tile_start < pad_end[-1], jnp.minimum(tile_e, N_EXPERTS - 1), last_e).astype(I32)
    return row_tok, tile_e, n_used, dest


def _pad_cols(a, n):
    return jnp.concatenate([a, jnp.zeros(a.shape[:-1] + (n - a.shape[-1],), a.dtype)], axis=-1)


def _head_pair_slabs(qt, tq):
    d, t = qt.shape
    hpg = MXU_DEPTH // HEAD_DIM
    shape = (N_HEADS, MXU_DEPTH, 1)
    rows = lax.broadcasted_iota(I32, shape, 1) + MXU_DEPTH * (lax.broadcasted_iota(I32, shape, 0) // hpg)
    own = (rows // HEAD_DIM) == lax.broadcasted_iota(I32, shape, 0)
    tiled = jnp.repeat(qt.reshape(d // MXU_DEPTH, MXU_DEPTH, t), hpg, axis=0)
    qz = jnp.where(own, tiled, jnp.zeros((), qt.dtype))
    nb = t // tq
    a = jnp.transpose(qz.reshape(N_HEADS // 2, 2, MXU_DEPTH, nb, tq), (0, 2, 3, 1, 4))
    return a.reshape(N_HEADS // 2, MXU_DEPTH, nb * 2 * tq)


def kernel(x_prompt, x_sample, c_prompt, c_sample, cache_k, cache_v, cache_kidx, page_table, w_ada, b_ada,
           g_pre_mix, w_in, gmlp_ln_g, gmlp_ln_b, gmlp_w_s, gmlp_b_s, w_out, g_post_mix, g_pre_ffn, w_router,
           b_router, w_gate_up, b_gate_up, w_down, b_down, g_post_ffn):
    depth = w_ada.shape[0]
    assert depth == 1 and x_prompt.shape[0] == 1 and x_sample.shape[1] == 1
    _, t, d = x_prompt.shape
    bd = x_sample.shape[0]
    n_pages = page_table.shape[1]
    n_past = n_pages * PAGE_SIZE
    d_attn = N_HEADS * HEAD_DIM
    l = 0
    row2 = lambda a: a.reshape(1, -1)

    c_all = jnp.concatenate([c_prompt, c_sample], axis=0)
    m_pad = -(-c_all.shape[0] // SUBLANES) * SUBLANES
    c_all = jnp.concatenate([c_all, jnp.zeros((m_pad - c_all.shape[0], d), F32)], axis=0)
    mod = _adaln(c_all, w_ada[l], b_ada[l])
    mod_p = [mod[0:1, i * d:(i + 1) * d] for i in range(6)]
    mod_s = [mod[1:1 + bd, i * d:(i + 1) * d] for i in range(6)]

    w_in_l = w_in[l]
    w_in_p = jnp.concatenate([w_in_l[:, :_C_KI], _pad_cols(w_in_l[:, 3072:3136], LANES),
                              _pad_cols(w_in_l[:, 3136:3144], LANES)], axis=1).astype(BF16)
    tril = jnp.tril(jnp.ones((CHUNK, CHUNK), dtype=bool))
    ws_chunk = jnp.where(tril[None], gmlp_w_s[l], 0.0).astype(BF16)
    gw = 512 // GMLP_GROUPS
    bs_chunk = jnp.repeat(jnp.transpose(gmlp_b_s[l]), gw, axis=1)
    ws_row = jnp.repeat(gmlp_w_s[l][:, 0, 0], gw).reshape(1, -1)
    bs_row = jnp.repeat(gmlp_b_s[l][:, 0], gw).reshape(1, -1)
    wo = w_out[l].astype(BF16)
    wo_a, wo_b = wo[:512], wo[512:]
    wr_p = _pad_cols(w_router[l], LANES).astype(BF16)
    br_p = jnp.concatenate([b_router[l], jnp.full((LANES - N_EXPERTS,), NEG_INF, F32)]).reshape(1, LANES)
    g_pre, g_pm, g_pf, g_po = row2(g_pre_mix[l]), row2(g_post_mix[l]), row2(g_pre_ffn[l]), row2(g_post_ffn[l])
    ln_g, ln_b = row2(gmlp_ln_g[l]), row2(gmlp_ln_b[l])

    xp = x_prompt[0]
    tq = 128
    (q_p, k_p, kb_p, v_p, vb_p, qi_p, ki_p, kib_p, wi_p, oa_p, _) = _premix(
        xp, mod_p[0], mod_p[1], g_pre, w_in_p, ln_g, ln_b, ws_chunk, bs_chunk, tm=256, chunked=True)
    ob_p = _dsa_prompt(_head_pair_slabs(jnp.transpose(q_p), tq), jnp.transpose(qi_p), jnp.transpose(wi_p),
                       kib_p, kb_p, jnp.transpose(vb_p), tq=tq, tk=min(512, t), tk3=min(512, t))
    x1_p, h2_p, se_p, sw_p = _postmix(oa_p, ob_p, wo_a, wo_b, xp, mod_p[2], mod_p[3], mod_p[4], g_pm, g_pf,
                                      wr_p, br_p, tm=256)

    xs_ = x_sample[:, 0]
    (q_s, k_s, _, v_s, _, qi_s, ki_s, _, wi_s, oa_s, vn_s) = _premix(
        xs_, mod_s[0], mod_s[1], g_pre, w_in_p, ln_g, ln_b, ws_row, bs_row, tm=bd, chunked=False)
    page_flat = page_table.reshape(-1)
    pg = 16 if n_pages % 16 == 0 else 1
    kidx_t = jnp.transpose(cache_kidx, (0, 1, 3, 2))
    k_t = jnp.transpose(cache_k, (0, 1, 3, 4, 2)).reshape(depth, -1, d_attn, PAGE_SIZE)
    v_t = jnp.transpose(cache_v, (0, 1, 3, 4, 2)).reshape(depth, -1, d_attn, PAGE_SIZE)
    scores, snew = _idx_sample(page_flat, qi_s.reshape(bd, IDX_HEADS, IDX_DIM), wi_s.reshape(bd, IDX_HEADS, 1),
                               ki_s.reshape(bd, 1, IDX_DIM), kidx_t, l, n_pages=n_pages, pg=pg)
    topk_s = min(TOPK_MAX, (n_past + 1) // 4)
    mask, mnew = _topk_sample(scores, snew, topk_s)
    head_of_lane = jnp.arange(d_attn, dtype=I32) // HEAD_DIM
    qz_s = jnp.where(head_of_lane[None, None, :] == jnp.arange(N_HEADS, dtype=I32)[None, :, None],
                     q_s[:, None, :], jnp.zeros((), BF16))
    pga = 16 if n_pages % 16 == 0 else (8 if n_pages % 8 == 0 else 1)
    ob_s = _attn_sample(page_flat, qz_s, k_s.reshape(bd, 1, d_attn), v_s.reshape(bd, 1, d_attn), mask, mnew,
                        k_t, v_t, l, n_pages=n_pages, pg=pga)
    x1_s, h2_s, se_s, sw_s = _postmix(oa_s, ob_s.reshape(bd, d_attn).astype(BF16), wo_a, wo_b, xs_, mod_s[2],
                                      mod_s[3], mod_s[4], g_pm, g_pf, wr_p, br_p, tm=bd)

    tmoe = 256
    h2_all = jnp.concatenate([h2_p, h2_s, jnp.zeros((1, d), F32)], axis=0)
    sel_e = jnp.concatenate([se_p[:, :TOP_K_EXPERTS], se_s[:, :TOP_K_EXPERTS]], axis=0)
    row_tok, tile_e, n_used, dest = _route(sel_e, tmoe)
    ys = _moe(tile_e, n_used, h2_all[row_tok], w_gate_up[l], b_gate_up[l], w_down[l], b_down[l], tmoe=tmoe)
    y_p = _final(ys[jnp.transpose(dest[:t])], sw_p, x1_p, mod_p[5], g_po, tm=256)
    y_s = _final(ys[jnp.transpose(dest[t:])], sw_s, x1_s, mod_s[5], g_po, tm=bd)

    hs = (N_HEADS, HEAD_DIM)
    return (y_p[None], y_s[:, None],
            k_p.reshape(1, 1, t, *hs), v_p.reshape(1, 1, t, *hs), ki_p.reshape(1, 1, t, IDX_DIM),
            k_s.reshape(1, bd, 1, *hs), v_s.reshape(1, bd, 1, *hs), ki_s.reshape(1, bd, 1, IDX_DIM),
            vn_s.reshape(1, bd, 1, -1))
```

```python
import functools
import math

import jax
import jax.numpy as jnp
from jax import lax
from jax.experimental import pallas as pl
from jax.experimental.pallas import tpu as pltpu

F32 = jnp.float32
BF16 = jnp.bfloat16
I32 = jnp.int32

EPS = 1e-6
N_HEADS = 8
HEAD_DIM = 64
IDX_HEADS = 8
IDX_DIM = 64
GMLP_GROUPS = 4
CHUNK = 128
TOPK_MAX = 256
PAGE_SIZE = 128
N_EXPERTS = 32
TOP_K_EXPERTS = 4
SWIGLU_LIMIT = 7.0
SWIGLU_ALPHA = 1.702
LOG2E = 1.4426950408889634

LANES = 128
SUBLANES = 8
MXU_DEPTH = 256
VMEM_LIMIT = 60000 * 1024
INT_MIN = -2147483648
INT_MAX = 2147483647
NEG_INF = float("-inf")


def _cparams(*sem):
    return pltpu.CompilerParams(dimension_semantics=sem, vmem_limit_bytes=VMEM_LIMIT)


def _gelu_tanh(x):
    return 0.5 * x * (1.0 + jnp.tanh(0.7978845608028654 * (x + 0.044715 * (x * x * x))))


def _rms(x, g):
    return x * lax.rsqrt(jnp.mean(x * x, axis=-1, keepdims=True) + EPS) * g


def _ordered_bits_to_float(u):
    bits = jnp.where(u >= 0, u, u ^ jnp.int32(0x7FFFFFFF))
    return lax.bitcast_convert_type(bits, F32)


def _adaln_kernel(c_ref, w_ref, b_ref, o_ref):
    c = c_ref[...]
    a = c * (1.0 / (1.0 + jnp.exp(-c)))
    o_ref[...] = jnp.dot(a.astype(BF16), w_ref[...].astype(BF16), preferred_element_type=F32) + b_ref[...]


def _adaln(c_all, w_ada, b_ada):
    m, d = c_all.shape
    n = w_ada.shape[1]
    tn = 1536
    return pl.pallas_call(
        _adaln_kernel,
        out_shape=jax.ShapeDtypeStruct((m, n), F32),
        grid=(n // tn,),
        in_specs=[pl.BlockSpec((m, d), lambda j: (0, 0)),
                  pl.BlockSpec((d, tn), lambda j: (0, j)),
                  pl.BlockSpec((1, tn), lambda j: (0, j))],
        out_specs=pl.BlockSpec((m, tn), lambda j: (0, j)),
        compiler_params=_cparams("arbitrary"),
        name="adaln",
    )(c_all, w_ada, b_ada.reshape(1, n))


_C_U, _C_V, _C_Q, _C_K, _C_VV, _C_QI, _C_KI, _C_WI, _C_END = 0, 512, 1024, 1536, 2048, 2560, 3072, 3200, 3328


def _premix_kernel(x_ref, sh_ref, sc_ref, g_ref, w_ref, lg_ref, lb_ref, ws_ref, bs_ref,
                   q_ref, k_ref, kb_ref, v_ref, vb_ref, qi_ref, ki_ref, kib_ref, wi_ref, oa_ref, vn_ref,
                   *, tm, tq, chunked):
    x = x_ref[...]
    h = _rms(x, g_ref[...]) * (1.0 + sc_ref[...]) + sh_ref[...]
    p = jnp.dot(h.astype(BF16), w_ref[...], preferred_element_type=F32)
    k = p[:, _C_K:_C_VV]
    vv = p[:, _C_VV:_C_QI]
    ki = p[:, _C_KI:_C_KI + IDX_DIM]
    qs = p[:, _C_Q:_C_K] * (HEAD_DIM ** -0.5 * LOG2E)
    if chunked:
        qt = jnp.transpose(qs)
        hpg = MXU_DEPTH // HEAD_DIM
        row_head = lax.broadcasted_iota(I32, (MXU_DEPTH, 1), 0) // HEAD_DIM
        for pr in range(N_HEADS // 2):
            g = (2 * pr) // hpg
            qg = qt[g * MXU_DEPTH:(g + 1) * MXU_DEPTH, :]
            for half in range(2):
                qh = jnp.where(row_head == (2 * pr + half) % hpg, qg, 0.0).astype(BF16)
                for bb in range(tm // tq):
                    q_ref[pr, :, (2 * bb + half) * tq:(2 * bb + half + 1) * tq] = qh[:, bb * tq:(bb + 1) * tq]
    else:
        q_ref[...] = qs.astype(BF16)
    k_ref[...] = k
    kb_ref[...] = k.astype(BF16)
    v_ref[...] = vv
    vb_ref[...] = vv.astype(BF16)
    qi_ref[...] = (p[:, _C_QI:_C_KI] * (IDX_DIM ** -0.5)).astype(BF16)
    ki_ref[...] = ki
    kib_ref[...] = ki.astype(BF16)
    wi_ref[...] = p[:, _C_WI:_C_WI + IDX_HEADS] * (IDX_HEADS ** -0.5)
    gu = _gelu_tanh(p[:, _C_U:_C_V])
    gv = _gelu_tanh(p[:, _C_V:_C_Q])
    mu = jnp.mean(gv, axis=-1, keepdims=True)
    dv = gv - mu
    var = jnp.mean(dv * dv, axis=-1, keepdims=True)
    vn = dv * lax.rsqrt(var + EPS) * lg_ref[...] + lb_ref[...]
    vn_ref[...] = vn
    if chunked:
        gw = vn.shape[1] // GMLP_GROUPS
        vnb = vn.astype(BF16)
        for r in range(tm // CHUNK):
            rs = slice(r * CHUNK, (r + 1) * CHUNK)
            for g in range(GMLP_GROUPS):
                cs = slice(g * gw, (g + 1) * gw)
                s = jnp.dot(ws_ref[g], vnb[rs, cs], preferred_element_type=F32) + bs_ref[:, cs]
                oa_ref[rs, cs] = (gu[rs, cs] * s).astype(BF16)
    else:
        oa_ref[...] = (gu * (vn * ws_ref[...] + bs_ref[...])).astype(BF16)


def _premix(x, shift, scale, g_pre, w_in_p, ln_g, ln_b, ws, bs, *, tm, tq, chunked):
    t, d = x.shape
    per_row = shift.shape[0] != 1
    mod_spec = pl.BlockSpec((tm, d), lambda i: (i, 0)) if per_row else pl.BlockSpec((1, d), lambda i: (0, 0))
    const2 = lambda a: pl.BlockSpec(a.shape, lambda i: (0, 0))
    ws_spec = pl.BlockSpec(ws.shape, lambda i: (0, 0, 0)) if chunked else const2(ws)
    row = lambda n: pl.BlockSpec((tm, n), lambda i: (i, 0))
    outs = [((t, 512), BF16), ((t, 512), F32), ((t, 512), BF16), ((t, 512), F32), ((t, 512), BF16),
            ((t, 512), BF16), ((t, IDX_DIM), F32), ((t, IDX_DIM), BF16), ((t, IDX_HEADS), F32),
            ((t, 512), BF16), ((t, 512), F32)]
    out_specs = [row(s[1]) for s, _ in outs]
    if chunked:
        outs[0] = ((N_HEADS // 2, MXU_DEPTH, 2 * t), BF16)
        out_specs[0] = pl.BlockSpec((N_HEADS // 2, MXU_DEPTH, 2 * tm), lambda i: (0, 0, i))
    return pl.pallas_call(
        functools.partial(_premix_kernel, tm=tm, tq=tq, chunked=chunked),
        out_shape=[jax.ShapeDtypeStruct(s, dt) for s, dt in outs],
        grid=(t // tm,),
        in_specs=[row(d), mod_spec, mod_spec, const2(g_pre), const2(w_in_p), const2(ln_g), const2(ln_b),
                  ws_spec, const2(bs)],
        out_specs=out_specs,
        compiler_params=_cparams("arbitrary"),
        name="premix_chunked" if chunked else "premix_rows",
    )(x, shift, scale, g_pre, w_in_p, ln_g, ln_b, ws, bs)


_CNT_ROWS = 64
_RED_ROWS = 32


def _dsa_prompt_kernel(qz2_ref, qit_ref, wt_ref, kidx_ref, k_ref, vt_ref, o_ref,
                       s_ref, m_ref, l_ref, acc_ref, lg_ref, p_ref, *, tq, tk1, tk, tk3, topk, nbits):
    q0 = pl.program_id(0) * tq
    n_chunks = (q0 + tq + tk - 1) // tk
    n_chunks3 = (q0 + tq + tk3 - 1) // tk3
    qpos = q0 + lax.broadcasted_iota(I32, (1, tq), 1)
    w = wt_ref[...]

    def key_pos(c, size):
        return c * size + lax.broadcasted_iota(I32, (size, 1), 0)

    def p1(c, carry):
        k0 = pl.multiple_of(c * tk1, tk1)
        kc = kidx_ref[pl.ds(k0, tk1), :]
        acc = jnp.zeros((tk1, tq), F32)
        for h in range(IDX_HEADS):
            d = jnp.dot(kc, qit_ref[h * IDX_DIM:(h + 1) * IDX_DIM, :], preferred_element_type=F32)
            acc = acc + w[h:h + 1, :] * jnp.maximum(d, 0.0)
        s_ref[pl.ds(k0, tk1), :] = jnp.where(key_pos(c, tk1) <= qpos, acc, NEG_INF)
        return carry

    lax.fori_loop(0, (q0 + tq + tk1 - 1) // tk1, p1, 0)

    def count(pred):
        def body(c, acc):
            k0 = pl.multiple_of(c * tk, tk)
            hit = jnp.where(pred(s_ref[pl.ds(k0, tk), :], c), 1.0, 0.0)
            return acc + jnp.sum(hit.reshape(tk // _CNT_ROWS, _CNT_ROWS, tq), axis=0)
        acc = lax.fori_loop(0, n_chunks, body, jnp.zeros((_CNT_ROWS, tq), F32))
        return jnp.sum(acc, axis=0, keepdims=True)

    kf = float(topk)
    select_all = (qpos + 1) <= topk
    cnt0 = count(lambda s, c: s >= 0.0)
    nonneg = cnt0 >= kf
    base0 = jnp.where(nonneg, jnp.int32(0), jnp.int32(INT_MIN))
    cntb0 = jnp.where(nonneg, cnt0, (qpos + 1).astype(F32))

    def bisect(b, state):
        base, cntb = state
        cand = base | jnp.left_shift(jnp.int32(1), 30 - b)
        thr = _ordered_bits_to_float(cand)
        c = count(lambda s, cc: s >= thr)
        ok = c >= kf
        return jnp.where(ok, cand, base), jnp.where(ok, c, cntb)

    base, cntb = lax.fori_loop(0, 31, bisect, (base0, cntb0))
    tau = jnp.where(select_all, NEG_INF, _ordered_bits_to_float(base))
    straddle = jnp.logical_and(cntb > kf, jnp.logical_not(select_all))

    def tie_search():
        need = kf - count(lambda s, c: s > tau)

        def step(b, j):
            cand = j | jnp.left_shift(jnp.int32(1), nbits - 1 - b)
            below = count(lambda s, c: jnp.logical_and(s == tau, key_pos(c, tk) < cand))
            return jnp.where(below < need, cand, j)
        return lax.fori_loop(0, nbits, step, jnp.zeros((1, tq), I32))

    any_straddle = jnp.max(jnp.where(straddle, 1, 0)) > 0
    j_tie = lax.cond(any_straddle, tie_search, lambda: jnp.zeros((1, tq), I32))
    j_tie = jnp.where(straddle, j_tie, jnp.int32(INT_MAX))

    m_ref[...] = jnp.full(m_ref.shape, NEG_INF, F32)
    l_ref[...] = jnp.zeros(l_ref.shape, F32)
    acc_ref[...] = jnp.zeros(acc_ref.shape, F32)
    hpg = MXU_DEPTH // HEAD_DIM
    col = lambda x, op: op(op(x.reshape(tk3 // _RED_ROWS, _RED_ROWS, tq), axis=0), axis=0, keepdims=True)

    def chunk_start(c):
        return pl.multiple_of(jnp.minimum(c, n_chunks3 - 1) * tk3, tk3)

    def logits(c, buf):
        k0 = chunk_start(c)
        s = s_ref[pl.ds(k0, tk3), :]
        kp = k0 + lax.broadcasted_iota(I32, (tk3, 1), 0)
        sel = jnp.logical_or(s > tau, jnp.logical_and(s == tau, kp <= j_tie))
        live = jnp.logical_and(kp <= qpos, c < n_chunks3)
        bias = jnp.where(jnp.logical_and(sel, live), 0.0, NEG_INF)
        cmax = []
        for pr in range(N_HEADS // 2):
            g = (2 * pr) // hpg
            kc = k_ref[pl.ds(k0, tk3), g * MXU_DEPTH:(g + 1) * MXU_DEPTH]
            lg2 = jnp.dot(kc, qz2_ref[pr], preferred_element_type=F32)
            for half in range(2):
                lg = lg2[:, half * tq:(half + 1) * tq] + bias
                lg_ref[buf, 2 * pr + half] = lg
                cmax.append(col(lg, jnp.max))
        return jnp.concatenate(cmax, axis=0)

    def softmax_pv(c, buf, cmax):
        k0 = chunk_start(c)
        m_old = m_ref[...]
        m_new = jnp.maximum(m_old, cmax)
        m_safe = jnp.where(m_new == NEG_INF, 0.0, m_new)
        alpha = jnp.exp2(m_old - m_safe)
        m_ref[...] = m_new
        psum = []
        for h in range(N_HEADS):
            p = jnp.exp2(lg_ref[buf, h] - m_safe[h:h + 1, :])
            p_ref[buf, h] = p.astype(BF16)
            psum.append(col(p, jnp.sum))
        l_ref[...] = alpha * l_ref[...] + jnp.concatenate(psum, axis=0)
        for h in range(N_HEADS):
            hs = slice(h * HEAD_DIM, (h + 1) * HEAD_DIM)
            pv = jnp.dot(vt_ref[hs, pl.ds(k0, tk3)], p_ref[buf, h], preferred_element_type=F32)
            acc_ref[hs, :] = alpha[h:h + 1, :] * acc_ref[hs, :] + pv

    def p3(j, cmax_even):
        cmax_odd = logits(2 * j + 1, 1)
        softmax_pv(2 * j, 0, cmax_even)
        cmax_even = logits(2 * j + 2, 0)
        softmax_pv(2 * j + 1, 1, cmax_odd)
        return cmax_even

    lax.fori_loop(0, (n_chunks3 + 1) // 2, p3, logits(0, 0))
    for h in range(N_HEADS):
        hs = slice(h * HEAD_DIM, (h + 1) * HEAD_DIM)
        acc_ref[hs, :] = acc_ref[hs, :] * (1.0 / l_ref[h:h + 1, :])
    o_ref[...] = jnp.transpose(acc_ref[...]).astype(BF16)


def _dsa_prompt(qz2, qit, wt, kidx_b, k_b, vt_b, *, tq, tk1, tk, tk3):
    t = k_b.shape[0]
    topk = min(TOPK_MAX, t // 4)
    nbits = max(1, math.ceil(math.log2(t)))
    resident = lambda a: pl.BlockSpec(a.shape, lambda i: (0,) * a.ndim, pipeline_mode=pl.Buffered(1))
    return pl.pallas_call(
        functools.partial(_dsa_prompt_kernel, tq=tq, tk1=tk1, tk=tk, tk3=tk3, topk=topk, nbits=nbits),
        out_shape=jax.ShapeDtypeStruct((t, N_HEADS * HEAD_DIM), BF16),
        grid=(t // tq,),
        in_specs=[pl.BlockSpec((N_HEADS // 2, MXU_DEPTH, 2 * tq), lambda i: (0, 0, i)),
                  pl.BlockSpec((IDX_HEADS * IDX_DIM, tq), lambda i: (0, i)),
                  pl.BlockSpec((IDX_HEADS, tq), lambda i: (0, i)),
                  resident(kidx_b), resident(k_b), resident(vt_b)],
        out_specs=pl.BlockSpec((tq, N_HEADS * HEAD_DIM), lambda i: (i, 0)),
        scratch_shapes=[pltpu.VMEM((t, tq), F32), pltpu.VMEM((N_HEADS, tq), F32),
                        pltpu.VMEM((N_HEADS, tq), F32), pltpu.VMEM((N_HEADS * HEAD_DIM, tq), F32),
                        pltpu.VMEM((2, N_HEADS, tk3, tq), F32), pltpu.VMEM((2, N_HEADS, tk3, tq), BF16)],
        compiler_params=_cparams("arbitrary"),
        name="dsa_prompt",
    )(qz2, qit, wt, kidx_b, k_b, vt_b)


def _idx_sample_kernel(pt_ref, q_ref, w_ref, kn_ref, *rest, pg):
    page_refs, s_ref, sn_ref = rest[:pg], rest[pg], rest[pg + 1]
    q = q_ref[...]
    w = w_ref[...]
    for r in range(pg):
        kp = page_refs[r][...].astype(BF16)
        d = jnp.dot(q, kp, preferred_element_type=F32)
        s_ref[r:r + 1, :] = jnp.sum(w * jnp.maximum(d, 0.0), axis=0, keepdims=True)
    kn = kn_ref[...].astype(BF16).astype(F32)
    dn = jnp.sum(q.astype(F32) * kn, axis=1, keepdims=True)
    sn = jnp.sum(w * jnp.maximum(dn, 0.0), axis=0, keepdims=True)
    sn_ref[...] = jnp.broadcast_to(sn, sn_ref.shape)


def _idx_sample(page_flat, qi_s, w_s, kidx_new, cache_kidx, layer, *, n_pages, pg):
    bd = qi_s.shape[0]
    page_spec = lambda r: pl.BlockSpec((None, None, IDX_DIM, PAGE_SIZE),
                                       lambda b, j, pt: (layer, pt[b * n_pages + j * pg + r], 0, 0))
    grid_spec = pltpu.PrefetchScalarGridSpec(
        num_scalar_prefetch=1,
        grid=(bd, n_pages // pg),
        in_specs=[pl.BlockSpec((None, IDX_HEADS, IDX_DIM), lambda b, j, pt: (b, 0, 0)),
                  pl.BlockSpec((None, IDX_HEADS, 1), lambda b, j, pt: (b, 0, 0)),
                  pl.BlockSpec((None, 1, IDX_DIM), lambda b, j, pt: (b, 0, 0))]
                 + [page_spec(r) for r in range(pg)],
        out_specs=[pl.BlockSpec((None, pg, PAGE_SIZE), lambda b, j, pt: (b, j, 0)),
                   pl.BlockSpec((None, 1, LANES), lambda b, j, pt: (b, 0, 0))],
    )
    return pl.pallas_call(
        functools.partial(_idx_sample_kernel, pg=pg),
        out_shape=[jax.ShapeDtypeStruct((bd, n_pages, PAGE_SIZE), F32),
                   jax.ShapeDtypeStruct((bd, 1, LANES), F32)],
        grid_spec=grid_spec,
        compiler_params=_cparams("arbitrary", "arbitrary"),
        name="idx_sample",
    )(page_flat, qi_s, w_s, kidx_new, *([cache_kidx] * pg))


def _topk_sample_kernel(s_ref, sn_ref, mask_ref, mnew_ref, *, topk, nbits, group):
    rows = range(group)
    s = [s_ref[g] for g in rows]
    sn = [sn_ref[g][:, 0:1] for g in rows]
    n_pg, width = s[0].shape
    n_past = n_pg * width
    kpos = lax.broadcasted_iota(I32, (n_pg, width), 0) * width + lax.broadcasted_iota(I32, (n_pg, width), 1)
    kf = float(topk)

    def count(pred_past, pred_new):
        c = jnp.sum(jnp.sum(jnp.where(pred_past, 1.0, 0.0), axis=0, keepdims=True), axis=1, keepdims=True)
        return c + jnp.where(pred_new, 1.0, 0.0)

    base0 = tuple(jnp.where(count(s[g] >= 0.0, sn[g] >= 0.0) >= kf, jnp.int32(0), jnp.int32(INT_MIN)) for g in rows)

    def bisect(b, base):
        out = []
        for g in rows:
            cand = base[g] | jnp.left_shift(jnp.int32(1), 30 - b)
            thr = _ordered_bits_to_float(cand)
            out.append(jnp.where(count(s[g] >= thr, sn[g] >= thr) >= kf, cand, base[g]))
        return tuple(out)

    tau = [_ordered_bits_to_float(u) for u in lax.fori_loop(0, 31, bisect, base0)]
    need = [kf - count(s[g] > tau[g], sn[g] > tau[g]) for g in rows]

    def step(b, j):
        out = []
        for g in rows:
            cand = j[g] | jnp.left_shift(jnp.int32(1), nbits - 1 - b)
            below = count(jnp.logical_and(s[g] == tau[g], kpos < cand),
                          jnp.logical_and(sn[g] == tau[g], n_past < cand))
            out.append(jnp.where(below < need[g], cand, j[g]))
        return tuple(out)

    j_tie = lax.fori_loop(0, nbits, step, tuple(jnp.zeros((1, 1), I32) for _ in rows))
    for g in rows:
        sel = jnp.logical_or(s[g] > tau[g], jnp.logical_and(s[g] == tau[g], kpos <= j_tie[g]))
        sel_new = jnp.logical_or(sn[g] > tau[g], jnp.logical_and(sn[g] == tau[g], n_past <= j_tie[g]))
        mask_ref[g] = jnp.where(sel, 1.0, 0.0)
        mnew_ref[g] = jnp.broadcast_to(jnp.where(sel_new, 1.0, 0.0), mnew_ref.shape[1:])


def _topk_sample(scores, snew, topk):
    bd, n_pages, _ = scores.shape
    nbits = max(1, math.ceil(math.log2(n_pages * PAGE_SIZE + 1)))
    group = 8 if bd % 8 == 0 else 1
    return pl.pallas_call(
        functools.partial(_topk_sample_kernel, topk=topk, nbits=nbits, group=group),
        out_shape=[jax.ShapeDtypeStruct(scores.shape, F32), jax.ShapeDtypeStruct(snew.shape, F32)],
        grid=(bd // group,),
        in_specs=[pl.BlockSpec((group, n_pages, PAGE_SIZE), lambda b: (b, 0, 0)),
                  pl.BlockSpec((group, 1, LANES), lambda b: (b, 0, 0))],
        out_specs=[pl.BlockSpec((group, n_pages, PAGE_SIZE), lambda b: (b, 0, 0)),
                   pl.BlockSpec((group, 1, LANES), lambda b: (b, 0, 0))],
        compiler_params=_cparams("arbitrary"),
        name="topk_sample",
    )(scores, snew)


def _attn_sample_kernel(pt_ref, qz_ref, kn_ref, vn_ref, mask_ref, mnew_ref, *rest, pg):
    k_refs, v_refs = rest[:pg], rest[pg:2 * pg]
    o_ref, m_ref, l_ref, acc_ref = rest[2 * pg:]
    j = pl.program_id(1)
    qz = qz_ref[...]
    d_attn = qz.shape[1]

    @pl.when(j == 0)
    def _():
        sel_new = mnew_ref[...][:, 0:1] > 0.0
        kn = kn_ref[...].astype(BF16).astype(F32)
        lg = jnp.sum(qz.astype(F32) * kn, axis=1, keepdims=True)
        m_ref[...] = jnp.where(sel_new, lg, NEG_INF)
        l_ref[...] = jnp.where(sel_new, jnp.ones_like(lg), 0.0)
        vn = vn_ref[...].astype(BF16).astype(F32)
        acc_ref[...] = jnp.where(sel_new, jnp.broadcast_to(vn, acc_ref.shape), 0.0)

    lgs = []
    for r in range(pg):
        lg = jnp.dot(qz, k_refs[r][...].astype(BF16), preferred_element_type=F32)
        lgs.append(jnp.where(mask_ref[r:r + 1, :] > 0.0, lg, NEG_INF))
    lg_max = lgs[0]
    for r in range(1, pg):
        lg_max = jnp.maximum(lg_max, lgs[r])
    m_old = m_ref[...]
    m_new = jnp.maximum(m_old, jnp.max(lg_max, axis=1, keepdims=True))
    m_safe = jnp.where(m_new == NEG_INF, 0.0, m_new)
    alpha = jnp.exp2(m_old - m_safe)
    p_sum = None
    pv = None
    for r in range(pg):
        p = jnp.exp2(lgs[r] - m_safe)
        p_sum = p if r == 0 else p_sum + p
        pv_r = lax.dot_general(p.astype(BF16), v_refs[r][...].astype(BF16), (((1,), (1,)), ((), ())),
                               preferred_element_type=F32)
        pv = pv_r if r == 0 else pv + pv_r
    l_ref[...] = alpha * l_ref[...] + jnp.sum(p_sum, axis=1, keepdims=True)
    acc_ref[...] = alpha * acc_ref[...] + pv
    m_ref[...] = m_new

    @pl.when(j == pl.num_programs(1) - 1)
    def _():
        head_of_lane = lax.broadcasted_iota(I32, (N_HEADS, d_attn), 1) // HEAD_DIM
        own = head_of_lane == lax.broadcasted_iota(I32, (N_HEADS, d_attn), 0)
        o = jnp.where(own, acc_ref[...] * (1.0 / l_ref[...]), 0.0)
        o_ref[...] = jnp.sum(o, axis=0, keepdims=True)


def _attn_sample(page_flat, qz_s, k_new, v_new, mask, mnew, cache_kt, cache_vt, layer, *, n_pages, pg):
    bd, _, d_attn = qz_s.shape
    page_spec = lambda r: pl.BlockSpec((None, None, d_attn, PAGE_SIZE),
                                       lambda b, j, pt: (layer, pt[b * n_pages + j * pg + r], 0, 0))
    per_b = lambda n, w: pl.BlockSpec((None, n, w), lambda b, j, pt: (b, 0, 0))
    grid_spec = pltpu.PrefetchScalarGridSpec(
        num_scalar_prefetch=1,
        grid=(bd, n_pages // pg),
        in_specs=[per_b(N_HEADS, d_attn), per_b(1, d_attn), per_b(1, d_attn),
                  pl.BlockSpec((None, pg, PAGE_SIZE), lambda b, j, pt: (b, j, 0)),
                  per_b(1, LANES)]
                 + [page_spec(r) for r in range(pg)] * 2,
        out_specs=per_b(1, d_attn),
        scratch_shapes=[pltpu.VMEM((N_HEADS, 1), F32), pltpu.VMEM((N_HEADS, 1), F32),
                        pltpu.VMEM((N_HEADS, d_attn), F32)],
    )
    return pl.pallas_call(
        functools.partial(_attn_sample_kernel, pg=pg),
        out_shape=jax.ShapeDtypeStruct((bd, 1, d_attn), F32),
        grid_spec=grid_spec,
        compiler_params=_cparams("arbitrary", "arbitrary"),
        name="attn_sample",
    )(page_flat, qz_s, k_new, v_new, mask, mnew, *([cache_kt] * pg), *([cache_vt] * pg))


def _postmix_kernel(oa_ref, ob_ref, woa_ref, wob_ref, x_ref, g1_ref, sh_ref, sc_ref, gpm_ref, gpf_ref,
                    wr_ref, br_ref, x1_ref, h2_ref, se_ref, sw_ref):
    mix = (jnp.dot(oa_ref[...], woa_ref[...], preferred_element_type=F32)
           + jnp.dot(ob_ref[...], wob_ref[...], preferred_element_type=F32))
    x1 = x_ref[...] + g1_ref[...] * _rms(mix, gpm_ref[...])
    x1_ref[...] = x1
    h2 = _rms(x1, gpf_ref[...]) * (1.0 + sc_ref[...]) + sh_ref[...]
    h2b = h2.astype(BF16)
    h2_ref[...] = h2
    logits = jnp.dot(h2b, wr_ref[...], preferred_element_type=F32) + br_ref[...]
    lane = lax.broadcasted_iota(I32, logits.shape, 1)
    lane_f = lane.astype(F32)
    se = jnp.zeros(logits.shape, F32)
    sw = jnp.zeros(logits.shape, F32)
    top = None
    denom = None
    for r in range(TOP_K_EXPERTS):
        m = jnp.max(logits, axis=1, keepdims=True)
        idx = jnp.min(jnp.where(logits == m, lane_f, float(LANES)), axis=1, keepdims=True)
        if r == 0:
            top = m
        e = jnp.exp(m - top)
        denom = e if r == 0 else denom + e
        se = jnp.where(lane == r, idx, se)
        sw = jnp.where(lane == r, e, sw)
        logits = jnp.where(lane_f == idx, NEG_INF, logits)
    se_ref[...] = se.astype(I32)
    sw_ref[...] = sw * (1.0 / denom)


def _postmix(out_a, out_b, wo_a, wo_b, x, gate1, shift2, scale2, g_pm, g_pf, wr_p, br_p, *, tm):
    t, d = x.shape
    per_row = gate1.shape[0] != 1
    mod_spec = pl.BlockSpec((tm, d), lambda i: (i, 0)) if per_row else pl.BlockSpec((1, d), lambda i: (0, 0))
    const2 = lambda a: pl.BlockSpec(a.shape, lambda i: (0, 0))
    row = lambda n: pl.BlockSpec((tm, n), lambda i: (i, 0))
    return pl.pallas_call(
        _postmix_kernel,
        out_shape=[jax.ShapeDtypeStruct((t, d), F32), jax.ShapeDtypeStruct((t, d), F32),
                   jax.ShapeDtypeStruct((t, LANES), I32), jax.ShapeDtypeStruct((t, LANES), F32)],
        grid=(t // tm,),
        in_specs=[row(out_a.shape[1]), row(out_b.shape[1]), const2(wo_a), const2(wo_b), row(d),
                  mod_spec, mod_spec, mod_spec, const2(g_pm), const2(g_pf), const2(wr_p), const2(br_p)],
        out_specs=[row(d), row(d), row(LANES), row(LANES)],
        compiler_params=_cparams("arbitrary"),
        name="postmix",
    )(out_a, out_b, wo_a, wo_b, x, gate1, shift2, scale2, g_pm, g_pf, wr_p, br_p)


def _moe_kernel(te_ref, nu_ref, x_ref, wgu_ref, bgu_ref, wd_ref, bd_ref, o_ref, wgu_b, wd_b, *, d_ff):
    i = pl.program_id(0)
    changed = jnp.logical_or(i == 0, te_ref[i] != te_ref[jnp.maximum(i - 1, 0)])

    @pl.when(changed)
    def _():
        rows = 128
        def cast(r, carry):
            r0 = pl.multiple_of(r * rows, rows)
            wgu_b[pl.ds(r0, rows), :] = wgu_ref[pl.ds(r0, rows), :].astype(BF16)
            wd_b[pl.ds(r0, rows), :] = wd_ref[pl.ds(r0, rows), :].astype(BF16)
            return carry
        lax.fori_loop(0, wgu_b.shape[0] // rows, cast, 0)

    @pl.when(i < nu_ref[0])
    def _():
        gu = jnp.dot(x_ref[...].astype(BF16), wgu_b[...], preferred_element_type=F32) + bgu_ref[...]
        gate = jnp.minimum(gu[:, :d_ff], SWIGLU_LIMIT)
        up = jnp.clip(gu[:, d_ff:], -SWIGLU_LIMIT, SWIGLU_LIMIT)
        a = (up + 1.0) * (gate * (1.0 / (1.0 + jnp.exp(-SWIGLU_ALPHA * gate))))
        o_ref[...] = jnp.dot(a.astype(BF16), wd_b[...], preferred_element_type=F32) + bd_ref[...]

    @pl.when(i >= nu_ref[0])
    def _():
        o_ref[...] = jnp.zeros(o_ref.shape, F32)


def _moe(tile_e, n_used, xs, w_gate_up, b_gate_up, w_down, b_down, *, tmoe):
    n_rows, d = xs.shape
    n_tiles = n_rows // tmoe
    d_ff = w_down.shape[1]
    assert w_gate_up.shape[1] == d and w_down.shape[1] == w_down.shape[2] == d
    grid_spec = pltpu.PrefetchScalarGridSpec(
        num_scalar_prefetch=2,
        grid=(n_tiles,),
        in_specs=[pl.BlockSpec((tmoe, d), lambda i, te, nu: (i, 0)),
                  pl.BlockSpec((None, d, 2 * d_ff), lambda i, te, nu: (te[i], 0, 0)),
                  pl.BlockSpec((None, 1, 2 * d_ff), lambda i, te, nu: (te[i], 0, 0)),
                  pl.BlockSpec((None, d_ff, d), lambda i, te, nu: (te[i], 0, 0)),
                  pl.BlockSpec((None, 1, d), lambda i, te, nu: (te[i], 0, 0))],
        out_specs=pl.BlockSpec((tmoe, d), lambda i, te, nu: (i, 0)),
        scratch_shapes=[pltpu.VMEM((d, 2 * d_ff), BF16), pltpu.VMEM((d_ff, d), BF16)],
    )
    return pl.pallas_call(
        functools.partial(_moe_kernel, d_ff=d_ff),
        out_shape=jax.ShapeDtypeStruct((n_rows, d), F32),
        grid_spec=grid_spec,
        compiler_params=_cparams("arbitrary"),
        name="moe",
    )(tile_e, n_used, xs, w_gate_up, b_gate_up.reshape(N_EXPERTS, 1, -1), w_down,
      b_down.reshape(N_EXPERTS, 1, -1))


def _final_kernel(y4_ref, sw_ref, x1_ref, g2_ref, gpf_ref, o_ref):
    sw = sw_ref[...]
    f = ((y4_ref[0] * sw[:, 0:1] + y4_ref[1] * sw[:, 1:2]) + (y4_ref[2] * sw[:, 2:3] + y4_ref[3] * sw[:, 3:4]))
    o_ref[...] = x1_ref[...] + g2_ref[...] * _rms(f, gpf_ref[...])


def _final(y4, sw, x1, gate2, g_post_ffn, *, tm):
    t, d = x1.shape
    per_row = gate2.shape[0] != 1
    mod_spec = pl.BlockSpec((tm, d), lambda i: (i, 0)) if per_row else pl.BlockSpec((1, d), lambda i: (0, 0))
    return pl.pallas_call(
        _final_kernel,
        out_shape=jax.ShapeDtypeStruct((t, d), F32),
        grid=(t // tm,),
        in_specs=[pl.BlockSpec((TOP_K_EXPERTS, tm, d), lambda i: (0, i, 0)),
                  pl.BlockSpec((tm, LANES), lambda i: (i, 0)),
                  pl.BlockSpec((tm, d), lambda i: (i, 0)), mod_spec,
                  pl.BlockSpec((1, d), lambda i: (0, 0))],
        out_specs=pl.BlockSpec((tm, d), lambda i: (i, 0)),
        compiler_params=_cparams("arbitrary"),
        name="final",
    )(y4, sw, x1, gate2, g_post_ffn)


def _route(sel_e, tmoe):
    n_tok = sel_e.shape[0]
    n_assign = n_tok * TOP_K_EXPERTS
    onehot = sel_e[:, :, None] == jnp.arange(N_EXPERTS, dtype=I32)[None, None, :]
    per_tok = jnp.sum(onehot.astype(I32), axis=1)
    before = jnp.cumsum(per_tok, axis=0) - per_tok
    counts = before[-1] + per_tok[-1]
    padded = (counts + tmoe - 1) // tmoe * tmoe
    pad_end = jnp.cumsum(padded)
    pad_start = pad_end - padded
    dest = jnp.sum(jnp.where(onehot, (before + pad_start[None, :])[:, None, :], 0), axis=2)
    n_tiles = -(-n_assign // tmoe) + N_EXPERTS
    n_rows = n_tiles * tmoe
    flat_tok = jnp.repeat(jnp.arange(n_tok, dtype=I32), TOP_K_EXPERTS)
    row_tok = jnp.full((n_rows,), n_tok, I32).at[dest.reshape(-1)].set(flat_tok)
    tile_start = jnp.arange(n_tiles, dtype=I32) * tmoe
    tile_e = jnp.sum((tile_start[:, None] >= pad_end[None, :]).astype(I32), axis=1)
    n_used = (pad_end[-1] // tmoe).astype(I32).reshape(1)
    last_e = jnp.max(jnp.where(counts > 0, jnp.arange(N_EXPERTS, dtype=I32), 0))
    tile_e = jnp.where(tile_start < pad_end[-1], jnp.minimum(tile_e, N_EXPERTS - 1), last_e).astype(I32)
    return row_tok, tile_e, n_used, dest


def _pad_cols(a, n):
    return jnp.concatenate([a, jnp.zeros(a.shape[:-1] + (n - a.shape[-1],), a.dtype)], axis=-1)


def kernel(x_prompt, x_sample, c_prompt, c_sample, cache_k, cache_v, cache_kidx, page_table, w_ada, b_ada,
           g_pre_mix, w_in, gmlp_ln_g, gmlp_ln_b, gmlp_w_s, gmlp_b_s, w_out, g_post_mix, g_pre_ffn, w_router,
           b_router, w_gate_up, b_gate_up, w_down, b_down, g_post_ffn):
    depth = w_ada.shape[0]
    assert depth == 1 and x_prompt.shape[0] == 1 and x_sample.shape[1] == 1
    _, t, d = x_prompt.shape
    bd = x_sample.shape[0]
    n_pages = page_table.shape[1]
    n_past = n_pages * PAGE_SIZE
    d_attn = N_HEADS * HEAD_DIM
    l = 0
    row2 = lambda a: a.reshape(1, -1)

    c_all = jnp.concatenate([c_prompt, c_sample], axis=0)
    m_pad = -(-c_all.shape[0] // SUBLANES) * SUBLANES
    c_all = jnp.concatenate([c_all, jnp.zeros((m_pad - c_all.shape[0], d), F32)], axis=0)
    mod = _adaln(c_all, w_ada[l], b_ada[l])
    mod_p = [mod[0:1, i * d:(i + 1) * d] for i in range(6)]
    mod_s = [mod[1:1 + bd, i * d:(i + 1) * d] for i in range(6)]

    w_in_l = w_in[l]
    w_in_p = jnp.concatenate([w_in_l[:, :_C_KI], _pad_cols(w_in_l[:, 3072:3136], LANES),
                              _pad_cols(w_in_l[:, 3136:3144], LANES)], axis=1).astype(BF16)
    tril = jnp.tril(jnp.ones((CHUNK, CHUNK), dtype=bool))
    ws_chunk = jnp.where(tril[None], gmlp_w_s[l], 0.0).astype(BF16)
    gw = 512 // GMLP_GROUPS
    bs_chunk = jnp.repeat(jnp.transpose(gmlp_b_s[l]), gw, axis=1)
    ws_row = jnp.repeat(gmlp_w_s[l][:, 0, 0], gw).reshape(1, -1)
    bs_row = jnp.repeat(gmlp_b_s[l][:, 0], gw).reshape(1, -1)
    wo = w_out[l].astype(BF16)
    wo_a, wo_b = wo[:512], wo[512:]
    wr_p = _pad_cols(w_router[l], LANES).astype(BF16)
    br_p = jnp.concatenate([b_router[l], jnp.full((LANES - N_EXPERTS,), NEG_INF, F32)]).reshape(1, LANES)
    g_pre, g_pm, g_pf, g_po = row2(g_pre_mix[l]), row2(g_post_mix[l]), row2(g_pre_ffn[l]), row2(g_post_ffn[l])
    ln_g, ln_b = row2(gmlp_ln_g[l]), row2(gmlp_ln_b[l])

    xp = x_prompt[0]
    tq = 128
    (qz2_p, k_p, kb_p, v_p, vb_p, qi_p, ki_p, kib_p, wi_p, oa_p, _) = _premix(
        xp, mod_p[0], mod_p[1], g_pre, w_in_p, ln_g, ln_b, ws_chunk, bs_chunk, tm=256, tq=tq, chunked=True)
    ob_p = _dsa_prompt(qz2_p, jnp.transpose(qi_p), jnp.transpose(wi_p), kib_p, kb_p, jnp.transpose(vb_p),
                       tq=tq, tk1=min(1024, t), tk=min(512, t), tk3=min(512, t))
    x1_p, h2_p, se_p, sw_p = _postmix(oa_p, ob_p, wo_a, wo_b, xp, mod_p[2], mod_p[3], mod_p[4], g_pm, g_pf,
                                      wr_p, br_p, tm=256)

    xs_ = x_sample[:, 0]
    (q_s, k_s, _, v_s, _, qi_s, ki_s, _, wi_s, oa_s, vn_s) = _premix(
        xs_, mod_s[0], mod_s[1], g_pre, w_in_p, ln_g, ln_b, ws_row, bs_row, tm=bd, tq=tq, chunked=False)
    page_flat = page_table.reshape(-1)
    pg = 16 if n_pages % 16 == 0 else 1
    kidx_t = jnp.transpose(cache_kidx, (0, 1, 3, 2))
    k_t = jnp.transpose(cache_k, (0, 1, 3, 4, 2)).reshape(depth, -1, d_attn, PAGE_SIZE)
    v_t = jnp.transpose(cache_v, (0, 1, 3, 4, 2)).reshape(depth, -1, d_attn, PAGE_SIZE)
    scores, snew = _idx_sample(page_flat, qi_s.reshape(bd, IDX_HEADS, IDX_DIM), wi_s.reshape(bd, IDX_HEADS, 1),
                               ki_s.reshape(bd, 1, IDX_DIM), kidx_t, l, n_pages=n_pages, pg=pg)
    topk_s = min(TOPK_MAX, (n_past + 1) // 4)
    mask, mnew = _topk_sample(scores, snew, topk_s)
    head_of_lane = jnp.arange(d_attn, dtype=I32) // HEAD_DIM
    qz_s = jnp.where(head_of_lane[None, None, :] == jnp.arange(N_HEADS, dtype=I32)[None, :, None],
                     q_s[:, None, :], jnp.zeros((), BF16))
    pga = 16 if n_pages % 16 == 0 else (8 if n_pages % 8 == 0 else 1)
    ob_s = _attn_sample(page_flat, qz_s, k_s.reshape(bd, 1, d_attn), v_s.reshape(bd, 1, d_attn), mask, mnew,
                        k_t, v_t, l, n_pages=n_pages, pg=pga)
    x1_s, h2_s, se_s, sw_s = _postmix(oa_s, ob_s.reshape(bd, d_attn).astype(BF16), wo_a, wo_b, xs_, mod_s[2],
                                      mod_s[3], mod_s[4], g_pm, g_pf, wr_p, br_p, tm=bd)

    tmoe = 256
    h2_all = jnp.concatenate([h2_p, h2_s, jnp.zeros((1, d), F32)], axis=0)
    sel_e = jnp.concatenate([se_p[:, :TOP_K_EXPERTS], se_s[:, :TOP_K_EXPERTS]], axis=0)
    row_tok, tile_e, n_used, dest = _route(sel_e, tmoe)
    ys = _moe(tile_e, n_used, h2_all[row_tok], w_gate_up[l], b_gate_up[l], w_down[l], b_down[l], tmoe=tmoe)
    y_p = _final(ys[jnp.transpose(dest[:t])], sw_p, x1_p, mod_p[5], g_po, tm=256)
    y_s = _final(ys[jnp.transpose(dest[t:])], sw_s, x1_s, mod_s[5], g_po, tm=bd)

    hs = (N_HEADS, HEAD_DIM)
    return (y_p[None], y_s[:, None],
            k_p.reshape(1, 1, t, *hs), v_p.reshape(1, 1, t, *hs), ki_p.reshape(1, 1, t, IDX_DIM),
            k_s.reshape(1, bd, 1, *hs), v_s.reshape(1, bd, 1, *hs), ki_s.reshape(1, bd, 1, IDX_DIM),
            vn_s.reshape(1, bd, 1, -1))
```

```python
import functools
import math

import jax
import jax.numpy as jnp
from jax import lax
from jax.experimental import pallas as pl
from jax.experimental.pallas import tpu as pltpu

F32 = jnp.float32
BF16 = jnp.bfloat16
I32 = jnp.int32

EPS = 1e-6
N_HEADS = 8
HEAD_DIM = 64
IDX_HEADS = 8
IDX_DIM = 64
GMLP_GROUPS = 4
CHUNK = 128
TOPK_MAX = 256
PAGE_SIZE = 128
N_EXPERTS = 32
TOP_K_EXPERTS = 4
SWIGLU_LIMIT = 7.0
SWIGLU_ALPHA = 1.702
LOG2E = 1.4426950408889634

LANES = 128
SUBLANES = 8
MXU_DEPTH = 256
VMEM_LIMIT = 60000 * 1024
INT_MIN = -2147483648
INT_MAX = 2147483647
NEG_INF = float("-inf")


def _cparams(*sem):
    return pltpu.CompilerParams(dimension_semantics=sem, vmem_limit_bytes=VMEM_LIMIT)


def _gelu_tanh(x):
    return 0.5 * x * (1.0 + jnp.tanh(0.7978845608028654 * (x + 0.044715 * (x * x * x))))


def _rms(x, g):
    return x * lax.rsqrt(jnp.mean(x * x, axis=-1, keepdims=True) + EPS) * g


def _ordered_bits_to_float(u):
    bits = jnp.where(u >= 0, u, u ^ jnp.int32(0x7FFFFFFF))
    return lax.bitcast_convert_type(bits, F32)


def _adaln_kernel(c_ref, w_ref, b_ref, o_ref):
    c = c_ref[...]
    a = c * (1.0 / (1.0 + jnp.exp(-c)))
    o_ref[...] = jnp.dot(a.astype(BF16), w_ref[...].astype(BF16), preferred_element_type=F32) + b_ref[...]


def _adaln(c_all, w_ada, b_ada):
    m, d = c_all.shape
    n = w_ada.shape[1]
    tn = 1536
    return pl.pallas_call(
        _adaln_kernel,
        out_shape=jax.ShapeDtypeStruct((m, n), F32),
        grid=(n // tn,),
        in_specs=[pl.BlockSpec((m, d), lambda j: (0, 0)),
                  pl.BlockSpec((d, tn), lambda j: (0, j)),
                  pl.BlockSpec((1, tn), lambda j: (0, j))],
        out_specs=pl.BlockSpec((m, tn), lambda j: (0, j)),
        compiler_params=_cparams("arbitrary"),
        name="adaln",
    )(c_all, w_ada, b_ada.reshape(1, n))


_C_U, _C_V, _C_Q, _C_K, _C_VV, _C_QI, _C_KI, _C_WI, _C_END = 0, 512, 1024, 1536, 2048, 2560, 3072, 3200, 3328


def _premix_kernel(x_ref, sh_ref, sc_ref, g_ref, w_ref, lg_ref, lb_ref, ws_ref, bs_ref,
                   q_ref, k_ref, kb_ref, v_ref, vb_ref, qi_ref, ki_ref, kib_ref, wi_ref, oa_ref, vn_ref,
                   *, tm, tq, chunked):
    x = x_ref[...]
    h = _rms(x, g_ref[...]) * (1.0 + sc_ref[...]) + sh_ref[...]
    p = jnp.dot(h.astype(BF16), w_ref[...], preferred_element_type=F32)
    k = p[:, _C_K:_C_VV]
    vv = p[:, _C_VV:_C_QI]
    ki = p[:, _C_KI:_C_KI + IDX_DIM]
    qs = p[:, _C_Q:_C_K] * (HEAD_DIM ** -0.5 * LOG2E)
    if chunked:
        qt = jnp.transpose(qs)
        hpg = MXU_DEPTH // HEAD_DIM
        row_head = lax.broadcasted_iota(I32, (MXU_DEPTH, 1), 0) // HEAD_DIM
        for pr in range(N_HEADS // 2):
            g = (2 * pr) // hpg
            qg = qt[g * MXU_DEPTH:(g + 1) * MXU_DEPTH, :]
            for half in range(2):
                qh = jnp.where(row_head == (2 * pr + half) % hpg, qg, 0.0).astype(BF16)
                for bb in range(tm // tq):
                    q_ref[pr, :, (2 * bb + half) * tq:(2 * bb + half + 1) * tq] = qh[:, bb * tq:(bb + 1) * tq]
    else:
        q_ref[...] = qs.astype(BF16)
    k_ref[...] = k
    kb_ref[...] = k.astype(BF16)
    v_ref[...] = vv
    vb_ref[...] = vv.astype(BF16)
    qi_ref[...] = (p[:, _C_QI:_C_KI] * (IDX_DIM ** -0.5)).astype(BF16)
    ki_ref[...] = ki
    kib_ref[...] = ki.astype(BF16)
    wi_ref[...] = p[:, _C_WI:_C_WI + IDX_HEADS] * (IDX_HEADS ** -0.5)
    gu = _gelu_tanh(p[:, _C_U:_C_V])
    gv = _gelu_tanh(p[:, _C_V:_C_Q])
    mu = jnp.mean(gv, axis=-1, keepdims=True)
    dv = gv - mu
    var = jnp.mean(dv * dv, axis=-1, keepdims=True)
    vn = dv * lax.rsqrt(var + EPS) * lg_ref[...] + lb_ref[...]
    vn_ref[...] = vn
    if chunked:
        gw = vn.shape[1] // GMLP_GROUPS
        vnb = vn.astype(BF16)
        for r in range(tm // CHUNK):
            rs = slice(r * CHUNK, (r + 1) * CHUNK)
            for g in range(GMLP_GROUPS):
                cs = slice(g * gw, (g + 1) * gw)
                s = jnp.dot(ws_ref[g], vnb[rs, cs], preferred_element_type=F32) + bs_ref[:, cs]
                oa_ref[rs, cs] = (gu[rs, cs] * s).astype(BF16)
    else:
        oa_ref[...] = (gu * (vn * ws_ref[...] + bs_ref[...])).astype(BF16)


def _premix(x, shift, scale, g_pre, w_in_p, ln_g, ln_b, ws, bs, *, tm, tq, chunked):
    t, d = x.shape
    per_row = shift.shape[0] != 1
    mod_spec = pl.BlockSpec((tm, d), lambda i: (i, 0)) if per_row else pl.BlockSpec((1, d), lambda i: (0, 0))
    const2 = lambda a: pl.BlockSpec(a.shape, lambda i: (0, 0))
    ws_spec = pl.BlockSpec(ws.shape, lambda i: (0, 0, 0)) if chunked else const2(ws)
    row = lambda n: pl.BlockSpec((tm, n), lambda i: (i, 0))
    outs = [((t, 512), BF16), ((t, 512), F32), ((t, 512), BF16), ((t, 512), F32), ((t, 512), BF16),
            ((t, 512), BF16), ((t, IDX_DIM), F32), ((t, IDX_DIM), BF16), ((t, IDX_HEADS), F32),
            ((t, 512), BF16), ((t, 512), F32)]
    out_specs = [row(s[1]) for s, _ in outs]
    if chunked:
        outs[0] = ((N_HEADS // 2, MXU_DEPTH, 2 * t), BF16)
        out_specs[0] = pl.BlockSpec((N_HEADS // 2, MXU_DEPTH, 2 * tm), lambda i: (0, 0, i))
    return pl.pallas_call(
        functools.partial(_premix_kernel, tm=tm, tq=tq, chunked=chunked),
        out_shape=[jax.ShapeDtypeStruct(s, dt) for s, dt in outs],
        grid=(t // tm,),
        in_specs=[row(d), mod_spec, mod_spec, const2(g_pre), const2(w_in_p), const2(ln_g), const2(ln_b),
                  ws_spec, const2(bs)],
        out_specs=out_specs,
        compiler_params=_cparams("arbitrary"),
        name="premix_chunked" if chunked else "premix_rows",
    )(x, shift, scale, g_pre, w_in_p, ln_g, ln_b, ws, bs)


_CNT_ROWS = 64
_RED_ROWS = 32


def _dsa_prompt_kernel(qz2_ref, qit_ref, wt_ref, kidx_ref, k_ref, vt_ref, o_ref,
                       s_ref, sh_ref, m_ref, l_ref, acc_ref, lg_ref, p_ref, *, tq, tk1, tk, tk3, topk, nbits):
    q0 = pl.program_id(0) * tq
    n_chunks = (q0 + tq + tk - 1) // tk
    n_chunks3 = (q0 + tq + tk3 - 1) // tk3
    qpos = q0 + lax.broadcasted_iota(I32, (1, tq), 1)
    w = wt_ref[...]

    def key_pos(c, size):
        return c * size + lax.broadcasted_iota(I32, (size, 1), 0)

    def p1(c, carry):
        k0 = pl.multiple_of(c * tk1, tk1)
        kc = kidx_ref[pl.ds(k0, tk1), :]
        acc = jnp.zeros((tk1, tq), F32)
        for h in range(IDX_HEADS):
            d = jnp.dot(kc, qit_ref[h * IDX_DIM:(h + 1) * IDX_DIM, :], preferred_element_type=F32)
            acc = acc + w[h:h + 1, :] * jnp.maximum(d, 0.0)
        acc = jnp.where(key_pos(c, tk1) <= qpos, acc, NEG_INF)
        s_ref[pl.ds(k0, tk1), :] = acc
        sh_ref[pl.ds(k0, tk1), :] = acc.astype(BF16)
        return carry

    lax.fori_loop(0, (q0 + tq + tk1 - 1) // tk1, p1, 0)

    def count(pred):
        def body(c, acc):
            k0 = pl.multiple_of(c * tk, tk)
            hit = jnp.where(pred(s_ref[pl.ds(k0, tk), :], c), 1.0, 0.0)
            return acc + jnp.sum(hit.reshape(tk // _CNT_ROWS, _CNT_ROWS, tq), axis=0)
        acc = lax.fori_loop(0, n_chunks, body, jnp.zeros((_CNT_ROWS, tq), F32))
        return jnp.sum(acc, axis=0, keepdims=True)

    one_b, zero_b = jnp.ones((), BF16), jnp.zeros((), BF16)

    def count_coarse(thr):
        def body(c, acc):
            k0 = pl.multiple_of(c * tk, tk)
            hit = jnp.where(sh_ref[pl.ds(k0, tk), :] >= thr, one_b, zero_b)
            parts = [hit[r * _CNT_ROWS:(r + 1) * _CNT_ROWS, :] for r in range(tk // _CNT_ROWS)]
            while len(parts) > 1:
                parts = [parts[i] + parts[i + 1] for i in range(0, len(parts), 2)]
            return acc + parts[0].astype(F32)
        acc = lax.fori_loop(0, n_chunks, body, jnp.zeros((_CNT_ROWS, tq), F32))
        return jnp.sum(acc, axis=0, keepdims=True)

    def coarse_bits(c16):
        return jnp.left_shift(jnp.where(c16 >= 0, c16, c16 ^ jnp.int32(0x7FFF)), 16)

    kf = float(topk)
    select_all = (qpos + 1) <= topk
    c16 = jnp.where(count_coarse(jnp.zeros((1, tq), BF16)) >= kf, jnp.int32(0), jnp.int32(-32768))

    def coarse_step(b, c16):
        cand = c16 | jnp.left_shift(jnp.int32(1), 14 - b)
        thr = lax.bitcast_convert_type(coarse_bits(cand), F32).astype(BF16)
        return jnp.where(count_coarse(thr) >= kf, cand, c16)

    t_bits = coarse_bits(lax.fori_loop(0, 15, coarse_step, c16))
    t_key = jnp.where(t_bits >= 0, t_bits, t_bits ^ jnp.int32(0x7FFFFFFF))
    t_exp = jnp.right_shift(t_bits, 23) & 0xFF
    edge = jnp.logical_and(jnp.logical_or(t_exp <= 1, t_exp >= 254), jnp.logical_not(select_all))

    def bracket_search():
        def step(_, st):
            lo, hi = st
            mid = lo + jnp.right_shift(hi - lo + 1, 1)
            thr = _ordered_bits_to_float(mid)
            ok = count(lambda s, cc: s >= thr) >= kf
            return jnp.where(ok, mid, lo), jnp.where(ok, hi, mid - 1)
        lo, _ = lax.fori_loop(0, 18, step, (t_key - 0x10001, t_key + 0x10001))
        return lo

    def full_search():
        cnt0 = count(lambda s, c: s >= 0.0)
        base0 = jnp.where(cnt0 >= kf, jnp.int32(0), jnp.int32(INT_MIN))

        def bisect(b, base):
            cand = base | jnp.left_shift(jnp.int32(1), 30 - b)
            thr = _ordered_bits_to_float(cand)
            return jnp.where(count(lambda s, cc: s >= thr) >= kf, cand, base)
        return lax.fori_loop(0, 31, bisect, base0)

    base = lax.cond(jnp.max(jnp.where(edge, 1, 0)) > 0, full_search, bracket_search)
    tau = jnp.where(select_all, NEG_INF, _ordered_bits_to_float(base))
    straddle = jnp.logical_and(count(lambda s, c: s >= tau) > kf, jnp.logical_not(select_all))

    def tie_search():
        need = kf - count(lambda s, c: s > tau)

        def step(b, j):
            cand = j | jnp.left_shift(jnp.int32(1), nbits - 1 - b)
            below = count(lambda s, c: jnp.logical_and(s == tau, key_pos(c, tk) < cand))
            return jnp.where(below < need, cand, j)
        return lax.fori_loop(0, nbits, step, jnp.zeros((1, tq), I32))

    any_straddle = jnp.max(jnp.where(straddle, 1, 0)) > 0
    j_tie = lax.cond(any_straddle, tie_search, lambda: jnp.zeros((1, tq), I32))
    j_tie = jnp.where(straddle, j_tie, jnp.int32(INT_MAX))

    m_ref[...] = jnp.full(m_ref.shape, NEG_INF, F32)
    l_ref[...] = jnp.zeros(l_ref.shape, F32)
    acc_ref[...] = jnp.zeros(acc_ref.shape, F32)
    hpg = MXU_DEPTH // HEAD_DIM
    col = lambda x, op: op(op(x.reshape(tk3 // _RED_ROWS, _RED_ROWS, tq), axis=0), axis=0, keepdims=True)

    def chunk_start(c):
        return pl.multiple_of(jnp.minimum(c, n_chunks3 - 1) * tk3, tk3)

    def logits(c, buf):
        k0 = chunk_start(c)
        s = s_ref[pl.ds(k0, tk3), :]
        kp = k0 + lax.broadcasted_iota(I32, (tk3, 1), 0)
        sel = jnp.logical_or(s > tau, jnp.logical_and(s == tau, kp <= j_tie))
        live = jnp.logical_and(kp <= qpos, c < n_chunks3)
        bias = jnp.where(jnp.logical_and(sel, live), 0.0, NEG_INF)
        cmax = []
        for pr in range(N_HEADS // 2):
            g = (2 * pr) // hpg
            kc = k_ref[pl.ds(k0, tk3), g * MXU_DEPTH:(g + 1) * MXU_DEPTH]
            lg2 = jnp.dot(kc, qz2_ref[pr], preferred_element_type=F32)
            for half in range(2):
                lg = lg2[:, half * tq:(half + 1) * tq] + bias
                lg_ref[buf, 2 * pr + half] = lg
                cmax.append(col(lg, jnp.max))
        return jnp.concatenate(cmax, axis=0)

    def softmax_pv(c, buf, cmax):
        k0 = chunk_start(c)
        m_old = m_ref[...]
        m_new = jnp.maximum(m_old, cmax)
        m_safe = jnp.where(m_new == NEG_INF, 0.0, m_new)
        alpha = jnp.exp2(m_old - m_safe)
        m_ref[...] = m_new
        psum = []
        for h in range(N_HEADS):
            p = jnp.exp2(lg_ref[buf, h] - m_safe[h:h + 1, :])
            p_ref[buf, h] = p.astype(BF16)
            psum.append(col(p, jnp.sum))
        l_ref[...] = alpha * l_ref[...] + jnp.concatenate(psum, axis=0)
        for h in range(N_HEADS):
            hs = slice(h * HEAD_DIM, (h + 1) * HEAD_DIM)
            pv = jnp.dot(vt_ref[hs, pl.ds(k0, tk3)], p_ref[buf, h], preferred_element_type=F32)
            acc_ref[hs, :] = alpha[h:h + 1, :] * acc_ref[hs, :] + pv

    def p3(j, cmax_even):
        cmax_odd = logits(2 * j + 1, 1)
        softmax_pv(2 * j, 0, cmax_even)
        cmax_even = logits(2 * j + 2, 0)
        softmax_pv(2 * j + 1, 1, cmax_odd)
        return cmax_even

    lax.fori_loop(0, (n_chunks3 + 1) // 2, p3, logits(0, 0))
    for h in range(N_HEADS):
        hs = slice(h * HEAD_DIM, (h + 1) * HEAD_DIM)
        acc_ref[hs, :] = acc_ref[hs, :] * (1.0 / l_ref[h:h + 1, :])
    o_ref[...] = jnp.transpose(acc_ref[...]).astype(BF16)


def _dsa_prompt(qz2, qit, wt, kidx_b, k_b, vt_b, *, tq, tk1, tk, tk3):
    t = k_b.shape[0]
    topk = min(TOPK_MAX, t // 4)
    nbits = max(1, math.ceil(math.log2(t)))
    resident = lambda a: pl.BlockSpec(a.shape, lambda i: (0,) * a.ndim, pipeline_mode=pl.Buffered(1))
    return pl.pallas_call(
        functools.partial(_dsa_prompt_kernel, tq=tq, tk1=tk1, tk=tk, tk3=tk3, topk=topk, nbits=nbits),
        out_shape=jax.ShapeDtypeStruct((t, N_HEADS * HEAD_DIM), BF16),
        grid=(t // tq,),
        in_specs=[pl.BlockSpec((N_HEADS // 2, MXU_DEPTH, 2 * tq), lambda i: (0, 0, i)),
                  pl.BlockSpec((IDX_HEADS * IDX_DIM, tq), lambda i: (0, i)),
                  pl.BlockSpec((IDX_HEADS, tq), lambda i: (0, i)),
                  resident(kidx_b), resident(k_b), resident(vt_b)],
        out_specs=pl.BlockSpec((tq, N_HEADS * HEAD_DIM), lambda i: (i, 0)),
        scratch_shapes=[pltpu.VMEM((t, tq), F32), pltpu.VMEM((t, tq), BF16), pltpu.VMEM((N_HEADS, tq), F32),
                        pltpu.VMEM((N_HEADS, tq), F32), pltpu.VMEM((N_HEADS * HEAD_DIM, tq), F32),
                        pltpu.VMEM((2, N_HEADS, tk3, tq), F32), pltpu.VMEM((2, N_HEADS, tk3, tq), BF16)],
        compiler_params=_cparams("arbitrary"),
        name="dsa_prompt",
    )(qz2, qit, wt, kidx_b, k_b, vt_b)


def _idx_sample_kernel(pt_ref, q_ref, w_ref, kn_ref, *rest, pg):
    page_refs, s_ref, sn_ref = rest[:pg], rest[pg], rest[pg + 1]
    q = q_ref[...]
    w = w_ref[...]
    for r in range(pg):
        kp = page_refs[r][...].astype(BF16)
        d = jnp.dot(q, kp, preferred_element_type=F32)
        s_ref[r:r + 1, :] = jnp.sum(w * jnp.maximum(d, 0.0), axis=0, keepdims=True)
    kn = kn_ref[...].astype(BF16).astype(F32)
    dn = jnp.sum(q.astype(F32) * kn, axis=1, keepdims=True)
    sn = jnp.sum(w * jnp.maximum(dn, 0.0), axis=0, keepdims=True)
    sn_ref[...] = jnp.broadcast_to(sn, sn_ref.shape)


def _idx_sample(page_flat, qi_s, w_s, kidx_new, cache_kidx, layer, *, n_pages, pg):
    bd = qi_s.shape[0]
    page_spec = lambda r: pl.BlockSpec((None, None, IDX_DIM, PAGE_SIZE),
                                       lambda b, j, pt: (layer, pt[b * n_pages + j * pg + r], 0, 0))
    grid_spec = pltpu.PrefetchScalarGridSpec(
        num_scalar_prefetch=1,
        grid=(bd, n_pages // pg),
        in_specs=[pl.BlockSpec((None, IDX_HEADS, IDX_DIM), lambda b, j, pt: (b, 0, 0)),
                  pl.BlockSpec((None, IDX_HEADS, 1), lambda b, j, pt: (b, 0, 0)),
                  pl.BlockSpec((None, 1, IDX_DIM), lambda b, j, pt: (b, 0, 0))]
                 + [page_spec(r) for r in range(pg)],
        out_specs=[pl.BlockSpec((None, pg, PAGE_SIZE), lambda b, j, pt: (b, j, 0)),
                   pl.BlockSpec((None, 1, LANES), lambda b, j, pt: (b, 0, 0))],
    )
    return pl.pallas_call(
        functools.partial(_idx_sample_kernel, pg=pg),
        out_shape=[jax.ShapeDtypeStruct((bd, n_pages, PAGE_SIZE), F32),
                   jax.ShapeDtypeStruct((bd, 1, LANES), F32)],
        grid_spec=grid_spec,
        compiler_params=_cparams("arbitrary", "arbitrary"),
        name="idx_sample",
    )(page_flat, qi_s, w_s, kidx_new, *([cache_kidx] * pg))


def _topk_sample_kernel(s_ref, sn_ref, mask_ref, mnew_ref, *, topk, nbits, group):
    rows = range(group)
    s = [s_ref[g] for g in rows]
    sn = [sn_ref[g][:, 0:1] for g in rows]
    n_pg, width = s[0].shape
    n_past = n_pg * width
    kpos = lax.broadcasted_iota(I32, (n_pg, width), 0) * width + lax.broadcasted_iota(I32, (n_pg, width), 1)
    kf = float(topk)

    def count(pred_past, pred_new):
        c = jnp.sum(jnp.sum(jnp.where(pred_past, 1.0, 0.0), axis=0, keepdims=True), axis=1, keepdims=True)
        return c + jnp.where(pred_new, 1.0, 0.0)

    base0 = tuple(jnp.where(count(s[g] >= 0.0, sn[g] >= 0.0) >= kf, jnp.int32(0), jnp.int32(INT_MIN)) for g in rows)

    def bisect(b, base):
        out = []
        for g in rows:
            cand = base[g] | jnp.left_shift(jnp.int32(1), 30 - b)
            thr = _ordered_bits_to_float(cand)
            out.append(jnp.where(count(s[g] >= thr, sn[g] >= thr) >= kf, cand, base[g]))
        return tuple(out)

    tau = [_ordered_bits_to_float(u) for u in lax.fori_loop(0, 31, bisect, base0)]
    need = [kf - count(s[g] > tau[g], sn[g] > tau[g]) for g in rows]

    def step(b, j):
        out = []
        for g in rows:
            cand = j[g] | jnp.left_shift(jnp.int32(1), nbits - 1 - b)
            below = count(jnp.logical_and(s[g] == tau[g], kpos < cand),
                          jnp.logical_and(sn[g] == tau[g], n_past < cand))
            out.append(jnp.where(below < need[g], cand, j[g]))
        return tuple(out)

    j_tie = lax.fori_loop(0, nbits, step, tuple(jnp.zeros((1, 1), I32) for _ in rows))
    for g in rows:
        sel = jnp.logical_or(s[g] > tau[g], jnp.logical_and(s[g] == tau[g], kpos <= j_tie[g]))
        sel_new = jnp.logical_or(sn[g] > tau[g], jnp.logical_and(sn[g] == tau[g], n_past <= j_tie[g]))
        mask_ref[g] = jnp.where(sel, 1.0, 0.0)
        mnew_ref[g] = jnp.broadcast_to(jnp.where(sel_new, 1.0, 0.0), mnew_ref.shape[1:])


def _topk_sample(scores, snew, topk):
    bd, n_pages, _ = scores.shape
    nbits = max(1, math.ceil(math.log2(n_pages * PAGE_SIZE + 1)))
    group = 8 if bd % 8 == 0 else 1
    return pl.pallas_call(
        functools.partial(_topk_sample_kernel, topk=topk, nbits=nbits, group=group),
        out_shape=[jax.ShapeDtypeStruct(scores.shape, F32), jax.ShapeDtypeStruct(snew.shape, F32)],
        grid=(bd // group,),
        in_specs=[pl.BlockSpec((group, n_pages, PAGE_SIZE), lambda b: (b, 0, 0)),
                  pl.BlockSpec((group, 1, LANES), lambda b: (b, 0, 0))],
        out_specs=[pl.BlockSpec((group, n_pages, PAGE_SIZE), lambda b: (b, 0, 0)),
                   pl.BlockSpec((group, 1, LANES), lambda b: (b, 0, 0))],
        compiler_params=_cparams("arbitrary"),
        name="topk_sample",
    )(scores, snew)


def _attn_sample_kernel(pt_ref, qz_ref, kn_ref, vn_ref, mask_ref, mnew_ref, *rest, pg):
    k_refs, v_refs = rest[:pg], rest[pg:2 * pg]
    o_ref, m_ref, l_ref, acc_ref = rest[2 * pg:]
    j = pl.program_id(1)
    qz = qz_ref[...]
    d_attn = qz.shape[1]

    @pl.when(j == 0)
    def _():
        sel_new = mnew_ref[...][:, 0:1] > 0.0
        kn = kn_ref[...].astype(BF16).astype(F32)
        lg = jnp.sum(qz.astype(F32) * kn, axis=1, keepdims=True)
        m_ref[...] = jnp.where(sel_new, lg, NEG_INF)
        l_ref[...] = jnp.where(sel_new, jnp.ones_like(lg), 0.0)
        vn = vn_ref[...].astype(BF16).astype(F32)
        acc_ref[...] = jnp.where(sel_new, jnp.broadcast_to(vn, acc_ref.shape), 0.0)

    lgs = []
    for r in range(pg):
        lg = jnp.dot(qz, k_refs[r][...].astype(BF16), preferred_element_type=F32)
        lgs.append(jnp.where(mask_ref[r:r + 1, :] > 0.0, lg, NEG_INF))
    lg_max = lgs[0]
    for r in range(1, pg):
        lg_max = jnp.maximum(lg_max, lgs[r])
    m_old = m_ref[...]
    m_new = jnp.maximum(m_old, jnp.max(lg_max, axis=1, keepdims=True))
    m_safe = jnp.where(m_new == NEG_INF, 0.0, m_new)
    alpha = jnp.exp2(m_old - m_safe)
    p_sum = None
    pv = None
    for r in range(pg):
        p = jnp.exp2(lgs[r] - m_safe)
        p_sum = p if r == 0 else p_sum + p
        pv_r = lax.dot_general(p.astype(BF16), v_refs[r][...].astype(BF16), (((1,), (1,)), ((), ())),
                               preferred_element_type=F32)
        pv = pv_r if r == 0 else pv + pv_r
    l_ref[...] = alpha * l_ref[...] + jnp.sum(p_sum, axis=1, keepdims=True)
    acc_ref[...] = alpha * acc_ref[...] + pv
    m_ref[...] = m_new

    @pl.when(j == pl.num_programs(1) - 1)
    def _():
        head_of_lane = lax.broadcasted_iota(I32, (N_HEADS, d_attn), 1) // HEAD_DIM
        own = head_of_lane == lax.broadcasted_iota(I32, (N_HEADS, d_attn), 0)
        o = jnp.where(own, acc_ref[...] * (1.0 / l_ref[...]), 0.0)
        o_ref[...] = jnp.sum(o, axis=0, keepdims=True)


def _attn_sample(page_flat, qz_s, k_new, v_new, mask, mnew, cache_kt, cache_vt, layer, *, n_pages, pg):
    bd, _, d_attn = qz_s.shape
    page_spec = lambda r: pl.BlockSpec((None, None, d_attn, PAGE_SIZE),
                                       lambda b, j, pt: (layer, pt[b * n_pages + j * pg + r], 0, 0))
    per_b = lambda n, w: pl.BlockSpec((None, n, w), lambda b, j, pt: (b, 0, 0))
    grid_spec = pltpu.PrefetchScalarGridSpec(
        num_scalar_prefetch=1,
        grid=(bd, n_pages // pg),
        in_specs=[per_b(N_HEADS, d_attn), per_b(1, d_attn), per_b(1, d_attn),
                  pl.BlockSpec((None, pg, PAGE_SIZE), lambda b, j, pt: (b, j, 0)),
                  per_b(1, LANES)]
                 + [page_spec(r) for r in range(pg)] * 2,
        out_specs=per_b(1, d_attn),
        scratch_shapes=[pltpu.VMEM((N_HEADS, 1), F32), pltpu.VMEM((N_HEADS, 1), F32),
                        pltpu.VMEM((N_HEADS, d_attn), F32)],
    )
    return pl.pallas_call(
        functools.partial(_attn_sample_kernel, pg=pg),
        out_shape=jax.ShapeDtypeStruct((bd, 1, d_attn), F32),
        grid_spec=grid_spec,
        compiler_params=_cparams("arbitrary", "arbitrary"),
        name="attn_sample",
    )(page_flat, qz_s, k_new, v_new, mask, mnew, *([cache_kt] * pg), *([cache_vt] * pg))


def _postmix_kernel(oa_ref, ob_ref, woa_ref, wob_ref, x_ref, g1_ref, sh_ref, sc_ref, gpm_ref, gpf_ref,
                    wr_ref, br_ref, x1_ref, h2_ref, se_ref, sw_ref):
    mix = (jnp.dot(oa_ref[...], woa_ref[...], preferred_element_type=F32)
           + jnp.dot(ob_ref[...], wob_ref[...], preferred_element_type=F32))
    x1 = x_ref[...] + g1_ref[...] * _rms(mix, gpm_ref[...])
    x1_ref[...] = x1
    h2 = _rms(x1, gpf_ref[...]) * (1.0 + sc_ref[...]) + sh_ref[...]
    h2b = h2.astype(BF16)
    h2_ref[...] = h2
    logits = jnp.dot(h2b, wr_ref[...], preferred_element_type=F32) + br_ref[...]
    lane = lax.broadcasted_iota(I32, logits.shape, 1)
    lane_f = lane.astype(F32)
    se = jnp.zeros(logits.shape, F32)
    sw = jnp.zeros(logits.shape, F32)
    top = None
    denom = None
    for r in range(TOP_K_EXPERTS):
        m = jnp.max(logits, axis=1, keepdims=True)
        idx = jnp.min(jnp.where(logits == m, lane_f, float(LANES)), axis=1, keepdims=True)
        if r == 0:
            top = m
        e = jnp.exp(m - top)
        denom = e if r == 0 else denom + e
        se = jnp.where(lane == r, idx, se)
        sw = jnp.where(lane == r, e, sw)
        logits = jnp.where(lane_f == idx, NEG_INF, logits)
    se_ref[...] = se.astype(I32)
    sw_ref[...] = sw * (1.0 / denom)


def _postmix(out_a, out_b, wo_a, wo_b, x, gate1, shift2, scale2, g_pm, g_pf, wr_p, br_p, *, tm):
    t, d = x.shape
    per_row = gate1.shape[0] != 1
    mod_spec = pl.BlockSpec((tm, d), lambda i: (i, 0)) if per_row else pl.BlockSpec((1, d), lambda i: (0, 0))
    const2 = lambda a: pl.BlockSpec(a.shape, lambda i: (0, 0))
    row = lambda n: pl.BlockSpec((tm, n), lambda i: (i, 0))
    return pl.pallas_call(
        _postmix_kernel,
        out_shape=[jax.ShapeDtypeStruct((t, d), F32), jax.ShapeDtypeStruct((t, d), F32),
                   jax.ShapeDtypeStruct((t, LANES), I32), jax.ShapeDtypeStruct((t, LANES), F32)],
        grid=(t // tm,),
        in_specs=[row(out_a.shape[1]), row(out_b.shape[1]), const2(wo_a), const2(wo_b), row(d),
                  mod_spec, mod_spec, mod_spec, const2(g_pm), const2(g_pf), const2(wr_p), const2(br_p)],
        out_specs=[row(d), row(d), row(LANES), row(LANES)],
        compiler_params=_cparams("arbitrary"),
        name="postmix",
    )(out_a, out_b, wo_a, wo_b, x, gate1, shift2, scale2, g_pm, g_pf, wr_p, br_p)


def _moe_kernel(te_ref, nu_ref, x_ref, wgu_ref, bgu_ref, wd_ref, bd_ref, o_ref, wgu_b, wd_b, *, d_ff):
    i = pl.program_id(0)
    changed = jnp.logical_or(i == 0, te_ref[i] != te_ref[jnp.maximum(i - 1, 0)])

    @pl.when(changed)
    def _():
        rows = 128
        def cast(r, carry):
            r0 = pl.multiple_of(r * rows, rows)
            wgu_b[pl.ds(r0, rows), :] = wgu_ref[pl.ds(r0, rows), :].astype(BF16)
            wd_b[pl.ds(r0, rows), :] = wd_ref[pl.ds(r0, rows), :].astype(BF16)
            return carry
        lax.fori_loop(0, wgu_b.shape[0] // rows, cast, 0)

    @pl.when(i < nu_ref[0])
    def _():
        gu = jnp.dot(x_ref[...].astype(BF16), wgu_b[...], preferred_element_type=F32) + bgu_ref[...]
        gate = jnp.minimum(gu[:, :d_ff], SWIGLU_LIMIT)
        up = jnp.clip(gu[:, d_ff:], -SWIGLU_LIMIT, SWIGLU_LIMIT)
        a = (up + 1.0) * (gate * (1.0 / (1.0 + jnp.exp(-SWIGLU_ALPHA * gate))))
        o_ref[...] = jnp.dot(a.astype(BF16), wd_b[...], preferred_element_type=F32) + bd_ref[...]

    @pl.when(i >= nu_ref[0])
    def _():
        o_ref[...] = jnp.zeros(o_ref.shape, F32)


def _moe(tile_e, n_used, xs, w_gate_up, b_gate_up, w_down, b_down, *, tmoe):
    n_rows, d = xs.shape
    n_tiles = n_rows // tmoe
    d_ff = w_down.shape[1]
    assert w_gate_up.shape[1] == d and w_down.shape[1] == w_down.shape[2] == d
    grid_spec = pltpu.PrefetchScalarGridSpec(
        num_scalar_prefetch=2,
        grid=(n_tiles,),
        in_specs=[pl.BlockSpec((tmoe, d), lambda i, te, nu: (i, 0)),
                  pl.BlockSpec((None, d, 2 * d_ff), lambda i, te, nu: (te[i], 0, 0)),
                  pl.BlockSpec((None, 1, 2 * d_ff), lambda i, te, nu: (te[i], 0, 0)),
                  pl.BlockSpec((None, d_ff, d), lambda i, te, nu: (te[i], 0, 0)),
                  pl.BlockSpec((None, 1, d), lambda i, te, nu: (te[i], 0, 0))],
        out_specs=pl.BlockSpec((tmoe, d), lambda i, te, nu: (i, 0)),
        scratch_shapes=[pltpu.VMEM((d, 2 * d_ff), BF16), pltpu.VMEM((d_ff, d), BF16)],
    )
    return pl.pallas_call(
        functools.partial(_moe_kernel, d_ff=d_ff),
        out_shape=jax.ShapeDtypeStruct((n_rows, d), F32),
        grid_spec=grid_spec,
        compiler_params=_cparams("arbitrary"),
        name="moe",
    )(tile_e, n_used, xs, w_gate_up, b_gate_up.reshape(N_EXPERTS, 1, -1), w_down,
      b_down.reshape(N_EXPERTS, 1, -1))


def _final_kernel(y4_ref, sw_ref, x1_ref, g2_ref, gpf_ref, o_ref):
    sw = sw_ref[...]
    f = ((y4_ref[0] * sw[:, 0:1] + y4_ref[1] * sw[:, 1:2]) + (y4_ref[2] * sw[:, 2:3] + y4_ref[3] * sw[:, 3:4]))
    o_ref[...] = x1_ref[...] + g2_ref[...] * _rms(f, gpf_ref[...])


def _final(y4, sw, x1, gate2, g_post_ffn, *, tm):
    t, d = x1.shape
    per_row = gate2.shape[0] != 1
    mod_spec = pl.BlockSpec((tm, d), lambda i: (i, 0)) if per_row else pl.BlockSpec((1, d), lambda i: (0, 0))
    return pl.pallas_call(
        _final_kernel,
        out_shape=jax.ShapeDtypeStruct((t, d), F32),
        grid=(t // tm,),
        in_specs=[pl.BlockSpec((TOP_K_EXPERTS, tm, d), lambda i: (0, i, 0)),
                  pl.BlockSpec((tm, LANES), lambda i: (i, 0)),
                  pl.BlockSpec((tm, d), lambda i: (i, 0)), mod_spec,
                  pl.BlockSpec((1, d), lambda i: (0, 0))],
        out_specs=pl.BlockSpec((tm, d), lambda i: (i, 0)),
        compiler_params=_cparams("arbitrary"),
        name="final",
    )(y4, sw, x1, gate2, g_post_ffn)


def _route(sel_e, tmoe):
    n_tok = sel_e.shape[0]
    n_assign = n_tok * TOP_K_EXPERTS
    onehot = sel_e[:, :, None] == jnp.arange(N_EXPERTS, dtype=I32)[None, None, :]
    per_tok = jnp.sum(onehot.astype(I32), axis=1)
    before = jnp.cumsum(per_tok, axis=0) - per_tok
    counts = before[-1] + per_tok[-1]
    padded = (counts + tmoe - 1) // tmoe * tmoe
    pad_end = jnp.cumsum(padded)
    pad_start = pad_end - padded
    dest = jnp.sum(jnp.where(onehot, (before + pad_start[None, :])[:, None, :], 0), axis=2)
    n_tiles = -(-n_assign // tmoe) + N_EXPERTS
    n_rows = n_tiles * tmoe
    flat_tok = jnp.repeat(jnp.arange(n_tok, dtype=I32), TOP_K_EXPERTS)
    row_tok = jnp.full((n_rows,), n_tok, I32).at[dest.reshape(-1)].set(flat_tok)
    tile_start = jnp.arange(n_tiles, dtype=I32) * tmoe
    tile_e = jnp.sum((tile_start[:, None] >= pad_end[None, :]).astype(I32), axis=1)
    n_used = (pad_end[-1] // tmoe).astype(I32).reshape(1)
    last_e = jnp.max(jnp.where(counts > 0, jnp.arange(N_EXPERTS, dtype=I32), 0))
    tile_e = jnp.where(tile_start < pad_end[-1], jnp.minimum(tile_e, N_EXPERTS - 1), last_e).astype(I32)
    return row_tok, tile_e, n_used, dest


def _pad_cols(a, n):
    return jnp.concatenate([a, jnp.zeros(a.shape[:-1] + (n - a.shape[-1],), a.dtype)], axis=-1)


def kernel(x_prompt, x_sample, c_prompt, c_sample, cache_k, cache_v, cache_kidx, page_table, w_ada, b_ada,
           g_pre_mix, w_in, gmlp_ln_g, gmlp_ln_b, gmlp_w_s, gmlp_b_s, w_out, g_post_mix, g_pre_ffn, w_router,
           b_router, w_gate_up, b_gate_up, w_down, b_down, g_post_ffn):
    depth = w_ada.shape[0]
    assert depth == 1 and x_prompt.shape[0] == 1 and x_sample.shape[1] == 1
    _, t, d = x_prompt.shape
    bd = x_sample.shape[0]
    n_pages = page_table.shape[1]
    n_past = n_pages * PAGE_SIZE
    d_attn = N_HEADS * HEAD_DIM
    l = 0
    row2 = lambda a: a.reshape(1, -1)

    c_all = jnp.concatenate([c_prompt, c_sample], axis=0)
    m_pad = -(-c_all.shape[0] // SUBLANES) * SUBLANES
    c_all = jnp.concatenate([c_all, jnp.zeros((m_pad - c_all.shape[0], d), F32)], axis=0)
    mod = _adaln(c_all, w_ada[l], b_ada[l])
    mod_p = [mod[0:1, i * d:(i + 1) * d] for i in range(6)]
    mod_s = [mod[1:1 + bd, i * d:(i + 1) * d] for i in range(6)]

    w_in_l = w_in[l]
    w_in_p = jnp.concatenate([w_in_l[:, :_C_KI], _pad_cols(w_in_l[:, 3072:3136], LANES),
                              _pad_cols(w_in_l[:, 3136:3144], LANES)], axis=1).astype(BF16)
    tril = jnp.tril(jnp.ones((CHUNK, CHUNK), dtype=bool))
    ws_chunk = jnp.where(tril[None], gmlp_w_s[l], 0.0).astype(BF16)
    gw = 512 // GMLP_GROUPS
    bs_chunk = jnp.repeat(jnp.transpose(gmlp_b_s[l]), gw, axis=1)
    ws_row = jnp.repeat(gmlp_w_s[l][:, 0, 0], gw).reshape(1, -1)
    bs_row = jnp.repeat(gmlp_b_s[l][:, 0], gw).reshape(1, -1)
    wo = w_out[l].astype(BF16)
    wo_a, wo_b = wo[:512], wo[512:]
    wr_p = _pad_cols(w_router[l], LANES).astype(BF16)
    br_p = jnp.concatenate([b_router[l], jnp.full((LANES - N_EXPERTS,), NEG_INF, F32)]).reshape(1, LANES)
    g_pre, g_pm, g_pf, g_po = row2(g_pre_mix[l]), row2(g_post_mix[l]), row2(g_pre_ffn[l]), row2(g_post_ffn[l])
    ln_g, ln_b = row2(gmlp_ln_g[l]), row2(gmlp_ln_b[l])

    xp = x_prompt[0]
    tq = 128
    (qz2_p, k_p, kb_p, v_p, vb_p, qi_p, ki_p, kib_p, wi_p, oa_p, _) = _premix(
        xp, mod_p[0], mod_p[1], g_pre, w_in_p, ln_g, ln_b, ws_chunk, bs_chunk, tm=256, tq=tq, chunked=True)
    ob_p = _dsa_prompt(qz2_p, jnp.transpose(qi_p), jnp.transpose(wi_p), kib_p, kb_p, jnp.transpose(vb_p),
                       tq=tq, tk1=min(1024, t), tk=min(512, t), tk3=min(512, t))
    x1_p, h2_p, se_p, sw_p = _postmix(oa_p, ob_p, wo_a, wo_b, xp, mod_p[2], mod_p[3], mod_p[4], g_pm, g_pf,
                                      wr_p, br_p, tm=256)

    xs_ = x_sample[:, 0]
    (q_s, k_s, _, v_s, _, qi_s, ki_s, _, wi_s, oa_s, vn_s) = _premix(
        xs_, mod_s[0], mod_s[1], g_pre, w_in_p, ln_g, ln_b, ws_row, bs_row, tm=bd, tq=tq, chunked=False)
    page_flat = page_table.reshape(-1)
    pg = 16 if n_pages % 16 == 0 else 1
    kidx_t = jnp.transpose(cache_kidx, (0, 1, 3, 2))
    k_t = jnp.transpose(cache_k, (0, 1, 3, 4, 2)).reshape(depth, -1, d_attn, PAGE_SIZE)
    v_t = jnp.transpose(cache_v, (0, 1, 3, 4, 2)).reshape(depth, -1, d_attn, PAGE_SIZE)
    scores, snew = _idx_sample(page_flat, qi_s.reshape(bd, IDX_HEADS, IDX_DIM), wi_s.reshape(bd, IDX_HEADS, 1),
                               ki_s.reshape(bd, 1, IDX_DIM), kidx_t, l, n_pages=n_pages, pg=pg)
    topk_s = min(TOPK_MAX, (n_past + 1) // 4)
    mask, mnew = _topk_sample(scores, snew, topk_s)
    head_of_lane = jnp.arange(d_attn, dtype=I32) // HEAD_DIM
    qz_s = jnp.where(head_of_lane[None, None, :] == jnp.arange(N_HEADS, dtype=I32)[None, :, None],
                     q_s[:, None, :], jnp.zeros((), BF16))
    pga = 16 if n_pages % 16 == 0 else (8 if n_pages % 8 == 0 else 1)
    ob_s = _attn_sample(page_flat, qz_s, k_s.reshape(bd, 1, d_attn), v_s.reshape(bd, 1, d_attn), mask, mnew,
                        k_t, v_t, l, n_pages=n_pages, pg=pga)
    x1_s, h2_s, se_s, sw_s = _postmix(oa_s, ob_s.reshape(bd, d_attn).astype(BF16), wo_a, wo_b, xs_, mod_s[2],
                                      mod_s[3], mod_s[4], g_pm, g_pf, wr_p, br_p, tm=bd)

    tmoe = 256
    h2_all = jnp.concatenate([h2_p, h2_s, jnp.zeros((1, d), F32)], axis=0)
    sel_e = jnp.concatenate([se_p[:, :TOP_K_EXPERTS], se_s[:, :TOP_K_EXPERTS]], axis=0)
    row_tok, tile_e, n_used, dest = _route(sel_e, tmoe)
    ys = _moe(tile_e, n_used, h2_all[row_tok], w_gate_up[l], b_gate_up[l], w_down[l], b_down[l], tmoe=tmoe)
    y_p = _final(ys[jnp.transpose(dest[:t])], sw_p, x1_p, mod_p[5], g_po, tm=256)
    y_s = _final(ys[jnp.transpose(dest[t:])], sw_s, x1_s, mod_s[5], g_po, tm=bd)

    hs = (N_HEADS, HEAD_DIM)
    return (y_p[None], y_s[:, None],
            k_p.reshape(1, 1, t, *hs), v_p.reshape(1, 1, t, *hs), ki_p.reshape(1, 1, t, IDX_DIM),
            k_s.reshape(1, bd, 1, *hs), v_s.reshape(1, bd, 1, *hs), ki_s.reshape(1, bd, 1, IDX_DIM),
            vn_s.reshape(1, bd, 1, -1))
```

```python
import functools
import math

import jax
import jax.numpy as jnp
from jax import lax
from jax.experimental import pallas as pl
from jax.experimental.pallas import tpu as pltpu

F32 = jnp.float32
BF16 = jnp.bfloat16
I32 = jnp.int32

EPS = 1e-6
N_HEADS = 8
HEAD_DIM = 64
IDX_HEADS = 8
IDX_DIM = 64
GMLP_GROUPS = 4
CHUNK = 128
TOPK_MAX = 256
PAGE_SIZE = 128
N_EXPERTS = 32
TOP_K_EXPERTS = 4
SWIGLU_LIMIT = 7.0
SWIGLU_ALPHA = 1.702
LOG2E = 1.4426950408889634

LANES = 128
SUBLANES = 8
MXU_DEPTH = 256
VMEM_LIMIT = 60000 * 1024
INT_MIN = -2147483648
INT_MAX = 2147483647
NEG_INF = float("-inf")


def _cparams(*sem):
    return pltpu.CompilerParams(dimension_semantics=sem, vmem_limit_bytes=VMEM_LIMIT)


def _gelu_tanh(x):
    return 0.5 * x * (1.0 + jnp.tanh(0.7978845608028654 * (x + 0.044715 * (x * x * x))))


def _rms(x, g):
    return x * lax.rsqrt(jnp.mean(x * x, axis=-1, keepdims=True) + EPS) * g


def _ordered_bits_to_float(u):
    bits = jnp.where(u >= 0, u, u ^ jnp.int32(0x7FFFFFFF))
    return lax.bitcast_convert_type(bits, F32)


def _adaln_kernel(c_ref, w_ref, b_ref, o_ref):
    c = c_ref[...]
    a = c * (1.0 / (1.0 + jnp.exp(-c)))
    o_ref[...] = jnp.dot(a.astype(BF16), w_ref[...].astype(BF16), preferred_element_type=F32) + b_ref[...]


def _adaln(c_all, w_ada, b_ada):
    m, d = c_all.shape
    n = w_ada.shape[1]
    tn = 1536
    return pl.pallas_call(
        _adaln_kernel,
        out_shape=jax.ShapeDtypeStruct((m, n), F32),
        grid=(n // tn,),
        in_specs=[pl.BlockSpec((m, d), lambda j: (0, 0)),
                  pl.BlockSpec((d, tn), lambda j: (0, j)),
                  pl.BlockSpec((1, tn), lambda j: (0, j))],
        out_specs=pl.BlockSpec((m, tn), lambda j: (0, j)),
        compiler_params=_cparams("arbitrary"),
        name="adaln",
    )(c_all, w_ada, b_ada.reshape(1, n))


_C_U, _C_V, _C_Q, _C_K, _C_VV, _C_QI, _C_KI, _C_WI, _C_END = 0, 512, 1024, 1536, 2048, 2560, 3072, 3200, 3328


def _premix_kernel(x_ref, sh_ref, sc_ref, g_ref, w_ref, lg_ref, lb_ref, ws_ref, bs_ref,
                   q_ref, k_ref, kb_ref, v_ref, vb_ref, qi_ref, ki_ref, kib_ref, wi_ref, oa_ref, vn_ref,
                   *, tm, tq, chunked):
    x = x_ref[...]
    h = _rms(x, g_ref[...]) * (1.0 + sc_ref[...]) + sh_ref[...]
    p = jnp.dot(h.astype(BF16), w_ref[...], preferred_element_type=F32)
    k = p[:, _C_K:_C_VV]
    vv = p[:, _C_VV:_C_QI]
    ki = p[:, _C_KI:_C_KI + IDX_DIM]
    qs = p[:, _C_Q:_C_K] * (HEAD_DIM ** -0.5 * LOG2E)
    if chunked:
        qt = jnp.transpose(qs)
        hpg = MXU_DEPTH // HEAD_DIM
        row_head = lax.broadcasted_iota(I32, (MXU_DEPTH, 1), 0) // HEAD_DIM
        for pr in range(N_HEADS // 2):
            g = (2 * pr) // hpg
            qg = qt[g * MXU_DEPTH:(g + 1) * MXU_DEPTH, :]
            for half in range(2):
                qh = jnp.where(row_head == (2 * pr + half) % hpg, qg, 0.0).astype(BF16)
                for bb in range(tm // tq):
                    q_ref[pr, :, (2 * bb + half) * tq:(2 * bb + half + 1) * tq] = qh[:, bb * tq:(bb + 1) * tq]
    else:
        q_ref[...] = qs.astype(BF16)
    k_ref[...] = k
    kb_ref[...] = k.astype(BF16)
    v_ref[...] = vv
    vb_ref[...] = vv.astype(BF16)
    qi_ref[...] = (p[:, _C_QI:_C_KI] * (IDX_DIM ** -0.5)).astype(BF16)
    ki_ref[...] = ki
    kib_ref[...] = ki.astype(BF16)
    wi_ref[...] = p[:, _C_WI:_C_WI + IDX_HEADS] * (IDX_HEADS ** -0.5)
    gu = _gelu_tanh(p[:, _C_U:_C_V])
    gv = _gelu_tanh(p[:, _C_V:_C_Q])
    mu = jnp.mean(gv, axis=-1, keepdims=True)
    dv = gv - mu
    var = jnp.mean(dv * dv, axis=-1, keepdims=True)
    vn = dv * lax.rsqrt(var + EPS) * lg_ref[...] + lb_ref[...]
    vn_ref[...] = vn
    if chunked:
        gw = vn.shape[1] // GMLP_GROUPS
        vnb = vn.astype(BF16)
        for r in range(tm // CHUNK):
            rs = slice(r * CHUNK, (r + 1) * CHUNK)
            for g in range(GMLP_GROUPS):
                cs = slice(g * gw, (g + 1) * gw)
                s = jnp.dot(ws_ref[g], vnb[rs, cs], preferred_element_type=F32) + bs_ref[:, cs]
                oa_ref[rs, cs] = (gu[rs, cs] * s).astype(BF16)
    else:
        oa_ref[...] = (gu * (vn * ws_ref[...] + bs_ref[...])).astype(BF16)


def _premix(x, shift, scale, g_pre, w_in_p, ln_g, ln_b, ws, bs, *, tm, tq, chunked):
    t, d = x.shape
    per_row = shift.shape[0] != 1
    mod_spec = pl.BlockSpec((tm, d), lambda i: (i, 0)) if per_row else pl.BlockSpec((1, d), lambda i: (0, 0))
    const2 = lambda a: pl.BlockSpec(a.shape, lambda i: (0, 0))
    ws_spec = pl.BlockSpec(ws.shape, lambda i: (0, 0, 0)) if chunked else const2(ws)
    row = lambda n: pl.BlockSpec((tm, n), lambda i: (i, 0))
    outs = [((t, 512), BF16), ((t, 512), F32), ((t, 512), BF16), ((t, 512), F32), ((t, 512), BF16),
            ((t, 512), BF16), ((t, IDX_DIM), F32), ((t, IDX_DIM), BF16), ((t, IDX_HEADS), F32),
            ((t, 512), BF16), ((t, 512), F32)]
    out_specs = [row(s[1]) for s, _ in outs]
    if chunked:
        outs[0] = ((N_HEADS // 2, MXU_DEPTH, 2 * t), BF16)
        out_specs[0] = pl.BlockSpec((N_HEADS // 2, MXU_DEPTH, 2 * tm), lambda i: (0, 0, i))
    return pl.pallas_call(
        functools.partial(_premix_kernel, tm=tm, tq=tq, chunked=chunked),
        out_shape=[jax.ShapeDtypeStruct(s, dt) for s, dt in outs],
        grid=(t // tm,),
        in_specs=[row(d), mod_spec, mod_spec, const2(g_pre), const2(w_in_p), const2(ln_g), const2(ln_b),
                  ws_spec, const2(bs)],
        out_specs=out_specs,
        compiler_params=_cparams("arbitrary"),
        name="premix_chunked" if chunked else "premix_rows",
    )(x, shift, scale, g_pre, w_in_p, ln_g, ln_b, ws, bs)


_CNT_ROWS = 64
_RED_ROWS = 32


def _dsa_prompt_kernel(qz2_ref, qit_ref, wt_ref, kidx_ref, k_ref, vt_ref, o_ref,
                       s_ref, m_ref, l_ref, acc_ref, lg_ref, p_ref, *, tq, tk1, tk, tk3, topk, nbits):
    q0 = pl.program_id(0) * tq
    n_chunks = (q0 + tq + tk - 1) // tk
    n_chunks3 = (q0 + tq + tk3 - 1) // tk3
    qpos = q0 + lax.broadcasted_iota(I32, (1, tq), 1)
    w = wt_ref[...]

    def key_pos(c, size):
        return c * size + lax.broadcasted_iota(I32, (size, 1), 0)

    def p1(c, carry):
        k0 = pl.multiple_of(c * tk1, tk1)
        kc = kidx_ref[pl.ds(k0, tk1), :]
        acc = jnp.zeros((tk1, tq), F32)
        for h in range(IDX_HEADS):
            d = jnp.dot(kc, qit_ref[h * IDX_DIM:(h + 1) * IDX_DIM, :], preferred_element_type=F32)
            acc = acc + w[h:h + 1, :] * jnp.maximum(d, 0.0)
        s_ref[pl.ds(k0, tk1), :] = jnp.where(key_pos(c, tk1) <= qpos, acc, NEG_INF)
        return carry

    lax.fori_loop(0, (q0 + tq + tk1 - 1) // tk1, p1, 0)

    def count(pred):
        def body(c, acc):
            k0 = pl.multiple_of(c * tk, tk)
            hit = jnp.where(pred(s_ref[pl.ds(k0, tk), :], c), 1.0, 0.0)
            return acc + jnp.sum(hit.reshape(tk // _CNT_ROWS, _CNT_ROWS, tq), axis=0)
        acc = lax.fori_loop(0, n_chunks, body, jnp.zeros((_CNT_ROWS, tq), F32))
        return jnp.sum(acc, axis=0, keepdims=True)

    kf = float(topk)
    select_all = (qpos + 1) <= topk
    cnt0 = count(lambda s, c: s >= 0.0)
    nonneg = cnt0 >= kf
    base0 = jnp.where(nonneg, jnp.int32(0), jnp.int32(INT_MIN))
    cntb0 = jnp.where(nonneg, cnt0, (qpos + 1).astype(F32))

    def bisect(b, state):
        base, cntb = state
        cand = base | jnp.left_shift(jnp.int32(1), 30 - b)
        thr = _ordered_bits_to_float(cand)
        c = count(lambda s, cc: s >= thr)
        ok = c >= kf
        return jnp.where(ok, cand, base), jnp.where(ok, c, cntb)

    base, cntb = lax.fori_loop(0, 31, bisect, (base0, cntb0))
    tau = jnp.where(select_all, NEG_INF, _ordered_bits_to_float(base))
    straddle = jnp.logical_and(cntb > kf, jnp.logical_not(select_all))

    def tie_search():
        need = kf - count(lambda s, c: s > tau)

        def step(b, j):
            cand = j | jnp.left_shift(jnp.int32(1), nbits - 1 - b)
            below = count(lambda s, c: jnp.logical_and(s == tau, key_pos(c, tk) < cand))
            return jnp.where(below < need, cand, j)
        return lax.fori_loop(0, nbits, step, jnp.zeros((1, tq), I32))

    any_straddle = jnp.max(jnp.where(straddle, 1, 0)) > 0
    j_tie = lax.cond(any_straddle, tie_search, lambda: jnp.zeros((1, tq), I32))
    j_tie = jnp.where(straddle, j_tie, jnp.int32(INT_MAX))

    m_ref[...] = jnp.full(m_ref.shape, NEG_INF, F32)
    l_ref[...] = jnp.zeros(l_ref.shape, F32)
    acc_ref[...] = jnp.zeros(acc_ref.shape, F32)
    hpg = MXU_DEPTH // HEAD_DIM
    col = lambda x, op: op(op(x.reshape(tk3 // _RED_ROWS, _RED_ROWS, tq), axis=0), axis=0, keepdims=True)

    def chunk_start(c):
        return pl.multiple_of(jnp.minimum(c, n_chunks3 - 1) * tk3, tk3)

    def logits(c, buf):
        k0 = chunk_start(c)
        s = s_ref[pl.ds(k0, tk3), :]
        kp = k0 + lax.broadcasted_iota(I32, (tk3, 1), 0)
        sel = jnp.logical_or(s > tau, jnp.logical_and(s == tau, kp <= j_tie))
        live = jnp.logical_and(kp <= qpos, c < n_chunks3)
        bias = jnp.where(jnp.logical_and(sel, live), 0.0, NEG_INF)
        cmax = []
        for pr in range(N_HEADS // 2):
            g = (2 * pr) // hpg
            kc = k_ref[pl.ds(k0, tk3), g * MXU_DEPTH:(g + 1) * MXU_DEPTH]
            lg2 = jnp.dot(kc, qz2_ref[pr], preferred_element_type=F32)
            for half in range(2):
                lg = lg2[:, half * tq:(half + 1) * tq] + bias
                lg_ref[buf, 2 * pr + half] = lg
                cmax.append(col(lg, jnp.max))
        return jnp.concatenate(cmax, axis=0)

    def softmax_pv(c, buf, cmax):
        k0 = chunk_start(c)
        m_old = m_ref[...]
        m_new = jnp.maximum(m_old, cmax)
        m_safe = jnp.where(m_new == NEG_INF, 0.0, m_new)
        alpha = jnp.exp2(m_old - m_safe)
        m_ref[...] = m_new
        psum = []
        for h in range(N_HEADS):
            p = jnp.exp2(lg_ref[buf, h] - m_safe[h:h + 1, :])
            p_ref[buf, h] = p.astype(BF16)
            psum.append(col(p, jnp.sum))
        l_ref[...] = alpha * l_ref[...] + jnp.concatenate(psum, axis=0)
        for h in range(N_HEADS):
            hs = slice(h * HEAD_DIM, (h + 1) * HEAD_DIM)
            pv = jnp.dot(vt_ref[hs, pl.ds(k0, tk3)], p_ref[buf, h], preferred_element_type=F32)
            acc_ref[hs, :] = alpha[h:h + 1, :] * acc_ref[hs, :] + pv

    def p3(j, cmax_even):
        cmax_odd = logits(2 * j + 1, 1)
        softmax_pv(2 * j, 0, cmax_even)
        cmax_even = logits(2 * j + 2, 0)
        softmax_pv(2 * j + 1, 1, cmax_odd)
        return cmax_even

    lax.fori_loop(0, (n_chunks3 + 1) // 2, p3, logits(0, 0))
    for h in range(N_HEADS):
        hs = slice(h * HEAD_DIM, (h + 1) * HEAD_DIM)
        acc_ref[hs, :] = acc_ref[hs, :] * (1.0 / l_ref[h:h + 1, :])
    o_ref[...] = jnp.transpose(acc_ref[...]).astype(BF16)


def _dsa_prompt(qz2, qit, wt, kidx_b, k_b, vt_b, *, tq, tk1, tk, tk3):
    t = k_b.shape[0]
    topk = min(TOPK_MAX, t // 4)
    nbits = max(1, math.ceil(math.log2(t)))
    resident = lambda a: pl.BlockSpec(a.shape, lambda i: (0,) * a.ndim, pipeline_mode=pl.Buffered(1))
    return pl.pallas_call(
        functools.partial(_dsa_prompt_kernel, tq=tq, tk1=tk1, tk=tk, tk3=tk3, topk=topk, nbits=nbits),
        out_shape=jax.ShapeDtypeStruct((t, N_HEADS * HEAD_DIM), BF16),
        grid=(t // tq,),
        in_specs=[pl.BlockSpec((N_HEADS // 2, MXU_DEPTH, 2 * tq), lambda i: (0, 0, i)),
                  pl.BlockSpec((IDX_HEADS * IDX_DIM, tq), lambda i: (0, i)),
                  pl.BlockSpec((IDX_HEADS, tq), lambda i: (0, i)),
                  resident(kidx_b), resident(k_b), resident(vt_b)],
        out_specs=pl.BlockSpec((tq, N_HEADS * HEAD_DIM), lambda i: (i, 0)),
        scratch_shapes=[pltpu.VMEM((t, tq), F32), pltpu.VMEM((N_HEADS, tq), F32),
                        pltpu.VMEM((N_HEADS, tq), F32), pltpu.VMEM((N_HEADS * HEAD_DIM, tq), F32),
                        pltpu.VMEM((2, N_HEADS, tk3, tq), F32), pltpu.VMEM((2, N_HEADS, tk3, tq), BF16)],
        compiler_params=_cparams("arbitrary"),
        name="dsa_prompt",
    )(qz2, qit, wt, kidx_b, k_b, vt_b)


def _idx_sample_kernel(pt_ref, q_ref, w_ref, kn_ref, *rest, pg):
    page_refs, s_ref, sn_ref = rest[:pg], rest[pg], rest[pg + 1]
    q = q_ref[...]
    w = w_ref[...]
    for r in range(pg):
        kp = page_refs[r][...].astype(BF16)
        d = jnp.dot(q, kp, preferred_element_type=F32)
        s_ref[r:r + 1, :] = jnp.sum(w * jnp.maximum(d, 0.0), axis=0, keepdims=True)
    kn = kn_ref[...].astype(BF16).astype(F32)
    dn = jnp.sum(q.astype(F32) * kn, axis=1, keepdims=True)
    sn = jnp.sum(w * jnp.maximum(dn, 0.0), axis=0, keepdims=True)
    sn_ref[...] = jnp.broadcast_to(sn, sn_ref.shape)


def _idx_sample(page_flat, qi_s, w_s, kidx_new, cache_kidx, layer, *, n_pages, pg):
    bd = qi_s.shape[0]
    page_spec = lambda r: pl.BlockSpec((None, None, IDX_DIM, PAGE_SIZE),
                                       lambda b, j, pt: (layer, pt[b * n_pages + j * pg + r], 0, 0))
    grid_spec = pltpu.PrefetchScalarGridSpec(
        num_scalar_prefetch=1,
        grid=(bd, n_pages // pg),
        in_specs=[pl.BlockSpec((None, IDX_HEADS, IDX_DIM), lambda b, j, pt: (b, 0, 0)),
                  pl.BlockSpec((None, IDX_HEADS, 1), lambda b, j, pt: (b, 0, 0)),
                  pl.BlockSpec((None, 1, IDX_DIM), lambda b, j, pt: (b, 0, 0))]
                 + [page_spec(r) for r in range(pg)],
        out_specs=[pl.BlockSpec((None, pg, PAGE_SIZE), lambda b, j, pt: (b, j, 0)),
                   pl.BlockSpec((None, 1, LANES), lambda b, j, pt: (b, 0, 0))],
    )
    return pl.pallas_call(
        functools.partial(_idx_sample_kernel, pg=pg),
        out_shape=[jax.ShapeDtypeStruct((bd, n_pages, PAGE_SIZE), F32),
                   jax.ShapeDtypeStruct((bd, 1, LANES), F32)],
        grid_spec=grid_spec,
        compiler_params=_cparams("arbitrary", "arbitrary"),
        name="idx_sample",
    )(page_flat, qi_s, w_s, kidx_new, *([cache_kidx] * pg))


def _topk_sample_kernel(s_ref, sn_ref, mask_ref, mnew_ref, *, topk, nbits, group):
    rows = range(group)
    s = [s_ref[g] for g in rows]
    sn = [sn_ref[g][:, 0:1] for g in rows]
    n_pg, width = s[0].shape
    n_past = n_pg * width
    kpos = lax.broadcasted_iota(I32, (n_pg, width), 0) * width + lax.broadcasted_iota(I32, (n_pg, width), 1)
    kf = float(topk)

    def count(pred_past, pred_new):
        c = jnp.sum(jnp.sum(jnp.where(pred_past, 1.0, 0.0), axis=0, keepdims=True), axis=1, keepdims=True)
        return c + jnp.where(pred_new, 1.0, 0.0)

    base0 = tuple(jnp.where(count(s[g] >= 0.0, sn[g] >= 0.0) >= kf, jnp.int32(0), jnp.int32(INT_MIN)) for g in rows)

    def bisect(b, base):
        out = []
        for g in rows:
            cand = base[g] | jnp.left_shift(jnp.int32(1), 30 - b)
            thr = _ordered_bits_to_float(cand)
            out.append(jnp.where(count(s[g] >= thr, sn[g] >= thr) >= kf, cand, base[g]))
        return tuple(out)

    tau = [_ordered_bits_to_float(u) for u in lax.fori_loop(0, 31, bisect, base0)]
    need = [kf - count(s[g] > tau[g], sn[g] > tau[g]) for g in rows]

    def step(b, j):
        out = []
        for g in rows:
            cand = j[g] | jnp.left_shift(jnp.int32(1), nbits - 1 - b)
            below = count(jnp.logical_and(s[g] == tau[g], kpos < cand),
                          jnp.logical_and(sn[g] == tau[g], n_past < cand))
            out.append(jnp.where(below < need[g], cand, j[g]))
        return tuple(out)

    j_tie = lax.fori_loop(0, nbits, step, tuple(jnp.zeros((1, 1), I32) for _ in rows))
    for g in rows:
        sel = jnp.logical_or(s[g] > tau[g], jnp.logical_and(s[g] == tau[g], kpos <= j_tie[g]))
        sel_new = jnp.logical_or(sn[g] > tau[g], jnp.logical_and(sn[g] == tau[g], n_past <= j_tie[g]))
        mask_ref[g] = jnp.where(sel, 1.0, 0.0)
        mnew_ref[g] = jnp.broadcast_to(jnp.where(sel_new, 1.0, 0.0), mnew_ref.shape[1:])


def _topk_sample(scores, snew, topk):
    bd, n_pages, _ = scores.shape
    nbits = max(1, math.ceil(math.log2(n_pages * PAGE_SIZE + 1)))
    group = 8 if bd % 8 == 0 else 1
    return pl.pallas_call(
        functools.partial(_topk_sample_kernel, topk=topk, nbits=nbits, group=group),
        out_shape=[jax.ShapeDtypeStruct(scores.shape, F32), jax.ShapeDtypeStruct(snew.shape, F32)],
        grid=(bd // group,),
        in_specs=[pl.BlockSpec((group, n_pages, PAGE_SIZE), lambda b: (b, 0, 0)),
                  pl.BlockSpec((group, 1, LANES), lambda b: (b, 0, 0))],
        out_specs=[pl.BlockSpec((group, n_pages, PAGE_SIZE), lambda b: (b, 0, 0)),
                   pl.BlockSpec((group, 1, LANES), lambda b: (b, 0, 0))],
        compiler_params=_cparams("arbitrary"),
        name="topk_sample",
    )(scores, snew)


def _attn_sample_kernel(pt_ref, qz_ref, kn_ref, vn_ref, mask_ref, mnew_ref, *rest, pg):
    k_refs, v_refs = rest[:pg], rest[pg:2 * pg]
    o_ref, m_ref, l_ref, acc_ref = rest[2 * pg:]
    j = pl.program_id(1)
    qz = qz_ref[...]
    d_attn = qz.shape[1]

    @pl.when(j == 0)
    def _():
        sel_new = mnew_ref[...][:, 0:1] > 0.0
        kn = kn_ref[...].astype(BF16).astype(F32)
        lg = jnp.sum(qz.astype(F32) * kn, axis=1, keepdims=True)
        m_ref[...] = jnp.where(sel_new, lg, NEG_INF)
        l_ref[...] = jnp.where(sel_new, jnp.ones_like(lg), 0.0)
        vn = vn_ref[...].astype(BF16).astype(F32)
        acc_ref[...] = jnp.where(sel_new, jnp.broadcast_to(vn, acc_ref.shape), 0.0)

    lgs = []
    for r in range(pg):
        lg = jnp.dot(qz, k_refs[r][...].astype(BF16), preferred_element_type=F32)
        lgs.append(jnp.where(mask_ref[r:r + 1, :] > 0.0, lg, NEG_INF))
    lg_max = lgs[0]
    for r in range(1, pg):
        lg_max = jnp.maximum(lg_max, lgs[r])
    m_old = m_ref[...]
    m_new = jnp.maximum(m_old, jnp.max(lg_max, axis=1, keepdims=True))
    m_safe = jnp.where(m_new == NEG_INF, 0.0, m_new)
    alpha = jnp.exp2(m_old - m_safe)
    p_sum = None
    pv = None
    for r in range(pg):
        p = jnp.exp2(lgs[r] - m_safe)
        p_sum = p if r == 0 else p_sum + p
        pv_r = lax.dot_general(p.astype(BF16), v_refs[r][...].astype(BF16), (((1,), (1,)), ((), ())),
                               preferred_element_type=F32)
        pv = pv_r if r == 0 else pv + pv_r
    l_ref[...] = alpha * l_ref[...] + jnp.sum(p_sum, axis=1, keepdims=True)
    acc_ref[...] = alpha * acc_ref[...] + pv
    m_ref[...] = m_new

    @pl.when(j == pl.num_programs(1) - 1)
    def _():
        head_of_lane = lax.broadcasted_iota(I32, (N_HEADS, d_attn), 1) // HEAD_DIM
        own = head_of_lane == lax.broadcasted_iota(I32, (N_HEADS, d_attn), 0)
        o = jnp.where(own, acc_ref[...] * (1.0 / l_ref[...]), 0.0)
        o_ref[...] = jnp.sum(o, axis=0, keepdims=True)


def _attn_sample(page_flat, qz_s, k_new, v_new, mask, mnew, cache_kt, cache_vt, layer, *, n_pages, pg):
    bd, _, d_attn = qz_s.shape
    page_spec = lambda r: pl.BlockSpec((None, None, d_attn, PAGE_SIZE),
                                       lambda b, j, pt: (layer, pt[b * n_pages + j * pg + r], 0, 0))
    per_b = lambda n, w: pl.BlockSpec((None, n, w), lambda b, j, pt: (b, 0, 0))
    grid_spec = pltpu.PrefetchScalarGridSpec(
        num_scalar_prefetch=1,
        grid=(bd, n_pages // pg),
        in_specs=[per_b(N_HEADS, d_attn), per_b(1, d_attn), per_b(1, d_attn),
                  pl.BlockSpec((None, pg, PAGE_SIZE), lambda b, j, pt: (b, j, 0)),
                  per_b(1, LANES)]
                 + [page_spec(r) for r in range(pg)] * 2,
        out_specs=per_b(1, d_attn),
        scratch_shapes=[pltpu.VMEM((N_HEADS, 1), F32), pltpu.VMEM((N_HEADS, 1), F32),
                        pltpu.VMEM((N_HEADS, d_attn), F32)],
    )
    return pl.pallas_call(
        functools.partial(_attn_sample_kernel, pg=pg),
        out_shape=jax.ShapeDtypeStruct((bd, 1, d_attn), F32),
        grid_spec=grid_spec,
        compiler_params=_cparams("arbitrary", "arbitrary"),
        name="attn_sample",
    )(page_flat, qz_s, k_new, v_new, mask, mnew, *([cache_kt] * pg), *([cache_vt] * pg))


def _postmix_kernel(oa_ref, ob_ref, woa_ref, wob_ref, x_ref, g1_ref, sh_ref, sc_ref, gpm_ref, gpf_ref,
                    wr_ref, br_ref, x1_ref, h2_ref, se_ref, sw_ref):
    mix = (jnp.dot(oa_ref[...], woa_ref[...], preferred_element_type=F32)
           + jnp.dot(ob_ref[...], wob_ref[...], preferred_element_type=F32))
    x1 = x_ref[...] + g1_ref[...] * _rms(mix, gpm_ref[...])
    x1_ref[...] = x1
    h2 = _rms(x1, gpf_ref[...]) * (1.0 + sc_ref[...]) + sh_ref[...]
    h2b = h2.astype(BF16)
    h2_ref[...] = h2
    logits = jnp.dot(h2b, wr_ref[...], preferred_element_type=F32) + br_ref[...]
    lane = lax.broadcasted_iota(I32, logits.shape, 1)
    lane_f = lane.astype(F32)
    se = jnp.zeros(logits.shape, F32)
    sw = jnp.zeros(logits.shape, F32)
    top = None
    denom = None
    for r in range(TOP_K_EXPERTS):
        m = jnp.max(logits, axis=1, keepdims=True)
        idx = jnp.min(jnp.where(logits == m, lane_f, float(LANES)), axis=1, keepdims=True)
        if r == 0:
            top = m
        e = jnp.exp(m - top)
        denom = e if r == 0 else denom + e
        se = jnp.where(lane == r, idx, se)
        sw = jnp.where(lane == r, e, sw)
        logits = jnp.where(lane_f == idx, NEG_INF, logits)
    se_ref[...] = se.astype(I32)
    sw_ref[...] = sw * (1.0 / denom)


def _postmix(out_a, out_b, wo_a, wo_b, x, gate1, shift2, scale2, g_pm, g_pf, wr_p, br_p, *, tm):
    t, d = x.shape
    per_row = gate1.shape[0] != 1
    mod_spec = pl.BlockSpec((tm, d), lambda i: (i, 0)) if per_row else pl.BlockSpec((1, d), lambda i: (0, 0))
    const2 = lambda a: pl.BlockSpec(a.shape, lambda i: (0, 0))
    row = lambda n: pl.BlockSpec((tm, n), lambda i: (i, 0))
    return pl.pallas_call(
        _postmix_kernel,
        out_shape=[jax.ShapeDtypeStruct((t, d), F32), jax.ShapeDtypeStruct((t, d), F32),
                   jax.ShapeDtypeStruct((t, LANES), I32), jax.ShapeDtypeStruct((t, LANES), F32)],
        grid=(t // tm,),
        in_specs=[row(out_a.shape[1]), row(out_b.shape[1]), const2(wo_a), const2(wo_b), row(d),
                  mod_spec, mod_spec, mod_spec, const2(g_pm), const2(g_pf), const2(wr_p), const2(br_p)],
        out_specs=[row(d), row(d), row(LANES), row(LANES)],
        compiler_params=_cparams("arbitrary"),
        name="postmix",
    )(out_a, out_b, wo_a, wo_b, x, gate1, shift2, scale2, g_pm, g_pf, wr_p, br_p)


def _moe_kernel(te_ref, nu_ref, x_ref, wgu_ref, bgu_ref, wd_ref, bd_ref, o_ref, wgu_b, wd_b, *, d_ff):
    i = pl.program_id(0)
    changed = jnp.logical_or(i == 0, te_ref[i] != te_ref[jnp.maximum(i - 1, 0)])

    @pl.when(changed)
    def _():
        rows = 128
        def cast(r, carry):
            r0 = pl.multiple_of(r * rows, rows)
            wgu_b[pl.ds(r0, rows), :] = wgu_ref[pl.ds(r0, rows), :].astype(BF16)
            wd_b[pl.ds(r0, rows), :] = wd_ref[pl.ds(r0, rows), :].astype(BF16)
            return carry
        lax.fori_loop(0, wgu_b.shape[0] // rows, cast, 0)

    @pl.when(i < nu_ref[0])
    def _():
        gu = jnp.dot(x_ref[...].astype(BF16), wgu_b[...], preferred_element_type=F32) + bgu_ref[...]
        gate = jnp.minimum(gu[:, :d_ff], SWIGLU_LIMIT)
        up = jnp.clip(gu[:, d_ff:], -SWIGLU_LIMIT, SWIGLU_LIMIT)
        a = (up + 1.0) * (gate * (1.0 / (1.0 + jnp.exp(-SWIGLU_ALPHA * gate))))
        o_ref[...] = jnp.dot(a.astype(BF16), wd_b[...], preferred_element_type=F32) + bd_ref[...]

    @pl.when(i >= nu_ref[0])
    def _():
        o_ref[...] = jnp.zeros(o_ref.shape, F32)


def _moe(tile_e, n_used, xs, w_gate_up, b_gate_up, w_down, b_down, *, tmoe):
    n_rows, d = xs.shape
    n_tiles = n_rows // tmoe
    d_ff = w_down.shape[1]
    assert w_gate_up.shape[1] == d and w_down.shape[1] == w_down.shape[2] == d
    grid_spec = pltpu.PrefetchScalarGridSpec(
        num_scalar_prefetch=2,
        grid=(n_tiles,),
        in_specs=[pl.BlockSpec((tmoe, d), lambda i, te, nu: (i, 0)),
                  pl.BlockSpec((None, d, 2 * d_ff), lambda i, te, nu: (te[i], 0, 0)),
                  pl.BlockSpec((None, 1, 2 * d_ff), lambda i, te, nu: (te[i], 0, 0)),
                  pl.BlockSpec((None, d_ff, d), lambda i, te, nu: (te[i], 0, 0)),
                  pl.BlockSpec((None, 1, d), lambda i, te, nu: (te[i], 0, 0))],
        out_specs=pl.BlockSpec((tmoe, d), lambda i, te, nu: (i, 0)),
        scratch_shapes=[pltpu.VMEM((d, 2 * d_ff), BF16), pltpu.VMEM((d_ff, d), BF16)],
    )
    return pl.pallas_call(
        functools.partial(_moe_kernel, d_ff=d_ff),
        out_shape=jax.ShapeDtypeStruct((n_rows, d), F32),
        grid_spec=grid_spec,
        compiler_params=_cparams("arbitrary"),
        name="moe",
    )(tile_e, n_used, xs, w_gate_up, b_gate_up.reshape(N_EXPERTS, 1, -1), w_down,
      b_down.reshape(N_EXPERTS, 1, -1))


def _final_kernel(y4_ref, sw_ref, x1_ref, g2_ref, gpf_ref, o_ref):
    sw = sw_ref[...]
    f = ((y4_ref[0] * sw[:, 0:1] + y4_ref[1] * sw[:, 1:2]) + (y4_ref[2] * sw[:, 2:3] + y4_ref[3] * sw[:, 3:4]))
    o_ref[...] = x1_ref[...] + g2_ref[...] * _rms(f, gpf_ref[...])


def _final(y4, sw, x1, gate2, g_post_ffn, *, tm):
    t, d = x1.shape
    per_row = gate2.shape[0] != 1
    mod_spec = pl.BlockSpec((tm, d), lambda i: (i, 0)) if per_row else pl.BlockSpec((1, d), lambda i: (0, 0))
    return pl.pallas_call(
        _final_kernel,
        out_shape=jax.ShapeDtypeStruct((t, d), F32),
        grid=(t // tm,),
        in_specs=[pl.BlockSpec((TOP_K_EXPERTS, tm, d), lambda i: (0, i, 0)),
                  pl.BlockSpec((tm, LANES), lambda i: (i, 0)),
                  pl.BlockSpec((tm, d), lambda i: (i, 0)), mod_spec,
                  pl.BlockSpec((1, d), lambda i: (0, 0))],
        out_specs=pl.BlockSpec((tm, d), lambda i: (i, 0)),
        compiler_params=_cparams("arbitrary"),
        name="final",
    )(y4, sw, x1, gate2, g_post_ffn)


def _route(sel_e, tmoe):
    n_tok = sel_e.shape[0]
    n_assign = n_tok * TOP_K_EXPERTS
    onehot = sel_e[:, :, None] == jnp.arange(N_EXPERTS, dtype=I32)[None, None, :]
    per_tok = jnp.sum(onehot.astype(I32), axis=1)
    before = jnp.cumsum(per_tok, axis=0) - per_tok
    counts = before[-1] + per_tok[-1]
    padded = (counts + tmoe - 1) // tmoe * tmoe
    pad_end = jnp.cumsum(padded)
    pad_start = pad_end - padded
    dest = jnp.sum(jnp.where(onehot, (before + pad_start[None, :])[:, None, :], 0), axis=2)
    n_tiles = -(-n_assign // tmoe) + N_EXPERTS
    n_rows = n_tiles * tmoe
    flat_tok = jnp.repeat(jnp.arange(n_tok, dtype=I32), TOP_K_EXPERTS)
    row_tok = jnp.full((n_rows,), n_tok, I32).at[dest.reshape(-1)].set(flat_tok, unique_indices=True)
    tile_start = jnp.arange(n_tiles, dtype=I32) * tmoe
    tile_e = jnp.sum((tile_start[:, None] >= pad_end[None, :]).astype(I32), axis=1)
    n_used = (pad_end[-1] // tmoe).astype(I32).reshape(1)
    last_e = jnp.max(jnp.where(counts > 0, jnp.arange(N_EXPERTS, dtype=I32), 0))
    tile_e = jnp.where(tile_start < pad_end[-1], jnp.minimum(tile_e, N_EXPERTS - 1), last_e).astype(I32)
    return row_tok, tile_e, n_used, dest


def _pad_cols(a, n):
    return jnp.concatenate([a, jnp.zeros(a.shape[:-1] + (n - a.shape[-1],), a.dtype)], axis=-1)


def kernel(x_prompt, x_sample, c_prompt, c_sample, cache_k, cache_v, cache_kidx, page_table, w_ada, b_ada,
           g_pre_mix, w_in, gmlp_ln_g, gmlp_ln_b, gmlp_w_s, gmlp_b_s, w_out, g_post_mix, g_pre_ffn, w_router,
           b_router, w_gate_up, b_gate_up, w_down, b_down, g_post_ffn):
    depth = w_ada.shape[0]
    assert depth == 1 and x_prompt.shape[0] == 1 and x_sample.shape[1] == 1
    _, t, d = x_prompt.shape
    bd = x_sample.shape[0]
    n_pages = page_table.shape[1]
    n_past = n_pages * PAGE_SIZE
    d_attn = N_HEADS * HEAD_DIM
    l = 0
    row2 = lambda a: a.reshape(1, -1)

    c_all = jnp.concatenate([c_prompt, c_sample], axis=0)
    m_pad = -(-c_all.shape[0] // SUBLANES) * SUBLANES
    c_all = jnp.concatenate([c_all, jnp.zeros((m_pad - c_all.shape[0], d), F32)], axis=0)
    mod = _adaln(c_all, w_ada[l], b_ada[l])
    mod_p = [mod[0:1, i * d:(i + 1) * d] for i in range(6)]
    mod_s = [mod[1:1 + bd, i * d:(i + 1) * d] for i in range(6)]

    w_in_l = w_in[l]
    w_in_p = jnp.concatenate([w_in_l[:, :_C_KI], _pad_cols(w_in_l[:, 3072:3136], LANES),
                              _pad_cols(w_in_l[:, 3136:3144], LANES)], axis=1).astype(BF16)
    tril = jnp.tril(jnp.ones((CHUNK, CHUNK), dtype=bool))
    ws_chunk = jnp.where(tril[None], gmlp_w_s[l], 0.0).astype(BF16)
    gw = 512 // GMLP_GROUPS
    bs_chunk = jnp.repeat(jnp.transpose(gmlp_b_s[l]), gw, axis=1)
    ws_row = jnp.repeat(gmlp_w_s[l][:, 0, 0], gw).reshape(1, -1)
    bs_row = jnp.repeat(gmlp_b_s[l][:, 0], gw).reshape(1, -1)
    wo = w_out[l].astype(BF16)
    wo_a, wo_b = wo[:512], wo[512:]
    wr_p = _pad_cols(w_router[l], LANES).astype(BF16)
    br_p = jnp.concatenate([b_router[l], jnp.full((LANES - N_EXPERTS,), NEG_INF, F32)]).reshape(1, LANES)
    g_pre, g_pm, g_pf, g_po = row2(g_pre_mix[l]), row2(g_post_mix[l]), row2(g_pre_ffn[l]), row2(g_post_ffn[l])
    ln_g, ln_b = row2(gmlp_ln_g[l]), row2(gmlp_ln_b[l])

    xp = x_prompt[0]
    tq = 128
    (qz2_p, k_p, kb_p, v_p, vb_p, qi_p, ki_p, kib_p, wi_p, oa_p, _) = _premix(
        xp, mod_p[0], mod_p[1], g_pre, w_in_p, ln_g, ln_b, ws_chunk, bs_chunk, tm=256, tq=tq, chunked=True)
    ob_p = _dsa_prompt(qz2_p, jnp.transpose(qi_p), jnp.transpose(wi_p), kib_p, kb_p, jnp.transpose(vb_p),
                       tq=tq, tk1=min(1024, t), tk=min(512, t), tk3=min(512, t))
    x1_p, h2_p, se_p, sw_p = _postmix(oa_p, ob_p, wo_a, wo_b, xp, mod_p[2], mod_p[3], mod_p[4], g_pm, g_pf,
                                      wr_p, br_p, tm=256)

    xs_ = x_sample[:, 0]
    (q_s, k_s, _, v_s, _, qi_s, ki_s, _, wi_s, oa_s, vn_s) = _premix(
        xs_, mod_s[0], mod_s[1], g_pre, w_in_p, ln_g, ln_b, ws_row, bs_row, tm=bd, tq=tq, chunked=False)
    page_flat = page_table.reshape(-1)
    pg = 32 if n_pages % 32 == 0 else (16 if n_pages % 16 == 0 else 1)
    kidx_t = jnp.transpose(cache_kidx, (0, 1, 3, 2))
    k_t = jnp.transpose(cache_k, (0, 1, 3, 4, 2)).reshape(depth, -1, d_attn, PAGE_SIZE)
    v_t = jnp.transpose(cache_v, (0, 1, 3, 4, 2)).reshape(depth, -1, d_attn, PAGE_SIZE)
    scores, snew = _idx_sample(page_flat, qi_s.reshape(bd, IDX_HEADS, IDX_DIM), wi_s.reshape(bd, IDX_HEADS, 1),
                               ki_s.reshape(bd, 1, IDX_DIM), kidx_t, l, n_pages=n_pages, pg=pg)
    topk_s = min(TOPK_MAX, (n_past + 1) // 4)
    mask, mnew = _topk_sample(scores, snew, topk_s)
    head_of_lane = jnp.arange(d_attn, dtype=I32) // HEAD_DIM
    qz_s = jnp.where(head_of_lane[None, None, :] == jnp.arange(N_HEADS, dtype=I32)[None, :, None],
                     q_s[:, None, :], jnp.zeros((), BF16))
    pga = 16 if n_pages % 16 == 0 else (8 if n_pages % 8 == 0 else 1)
    ob_s = _attn_sample(page_flat, qz_s, k_s.reshape(bd, 1, d_attn), v_s.reshape(bd, 1, d_attn), mask, mnew,
                        k_t, v_t, l, n_pages=n_pages, pg=pga)
    x1_s, h2_s, se_s, sw_s = _postmix(oa_s, ob_s.reshape(bd, d_attn).astype(BF16), wo_a, wo_b, xs_, mod_s[2],
                                      mod_s[3], mod_s[4], g_pm, g_pf, wr_p, br_p, tm=bd)

    tmoe = 256
    h2_all = jnp.concatenate([h2_p, h2_s, jnp.zeros((1, d), F32)], axis=0)
    sel_e = jnp.concatenate([se_p[:, :TOP_K_EXPERTS], se_s[:, :TOP_K_EXPERTS]], axis=0)
    row_tok, tile_e, n_used, dest = _route(sel_e, tmoe)
    ys = _moe(tile_e, n_used, h2_all[row_tok], w_gate_up[l], b_gate_up[l], w_down[l], b_down[l], tmoe=tmoe)
    y_p = _final(ys[jnp.transpose(dest[:t])], sw_p, x1_p, mod_p[5], g_po, tm=256)
    y_s = _final(ys[jnp.transpose(dest[t:])], sw_s, x1_s, mod_s[5], g_po, tm=bd)

    hs = (N_HEADS, HEAD_DIM)
    return (y_p[None], y_s[:, None],
            k_p.reshape(1, 1, t, *hs), v_p.reshape(1, 1, t, *hs), ki_p.reshape(1, 1, t, IDX_DIM),
            k_s.reshape(1, bd, 1, *hs), v_s.reshape(1, bd, 1, *hs), ki_s.reshape(1, bd, 1, IDX_DIM),
            vn_s.reshape(1, bd, 1, -1))
```

```python
import functools
import math

import jax
import jax.numpy as jnp
from jax import lax
from jax.experimental import pallas as pl
from jax.experimental.pallas import tpu as pltpu

F32 = jnp.float32
BF16 = jnp.bfloat16
I32 = jnp.int32

EPS = 1e-6
N_HEADS = 8
HEAD_DIM = 64
IDX_HEADS = 8
IDX_DIM = 64
GMLP_GROUPS = 4
CHUNK = 128
TOPK_MAX = 256
PAGE_SIZE = 128
N_EXPERTS = 32
TOP_K_EXPERTS = 4
SWIGLU_LIMIT = 7.0
SWIGLU_ALPHA = 1.702
LOG2E = 1.4426950408889634

LANES = 128
SUBLANES = 8
MXU_DEPTH = 256
VMEM_LIMIT = 60000 * 1024
INT_MIN = -2147483648
INT_MAX = 2147483647
NEG_INF = float("-inf")


def _cparams(*sem):
    return pltpu.CompilerParams(dimension_semantics=sem, vmem_limit_bytes=VMEM_LIMIT)


def _gelu_tanh(x):
    return 0.5 * x * (1.0 + jnp.tanh(0.7978845608028654 * (x + 0.044715 * (x * x * x))))


def _rms(x, g):
    return x * lax.rsqrt(jnp.mean(x * x, axis=-1, keepdims=True) + EPS) * g


def _ordered_bits_to_float(u):
    bits = jnp.where(u >= 0, u, u ^ jnp.int32(0x7FFFFFFF))
    return lax.bitcast_convert_type(bits, F32)


def _adaln_kernel(c_ref, w_ref, b_ref, o_ref):
    c = c_ref[...]
    a = c * (1.0 / (1.0 + jnp.exp(-c)))
    o_ref[...] = jnp.dot(a.astype(BF16), w_ref[...].astype(BF16), preferred_element_type=F32) + b_ref[...]


def _adaln(c_all, w_ada, b_ada):
    m, d = c_all.shape
    n = w_ada.shape[1]
    tn = 1536
    return pl.pallas_call(
        _adaln_kernel,
        out_shape=jax.ShapeDtypeStruct((m, n), F32),
        grid=(n // tn,),
        in_specs=[pl.BlockSpec((m, d), lambda j: (0, 0)),
                  pl.BlockSpec((d, tn), lambda j: (0, j)),
                  pl.BlockSpec((1, tn), lambda j: (0, j))],
        out_specs=pl.BlockSpec((m, tn), lambda j: (0, j)),
        compiler_params=_cparams("arbitrary"),
        name="adaln",
    )(c_all, w_ada, b_ada.reshape(1, n))


_C_U, _C_V, _C_Q, _C_K, _C_VV, _C_QI, _C_KI, _C_WI, _C_END = 0, 512, 1024, 1536, 2048, 2560, 3072, 3200, 3328


def _premix_kernel(x_ref, sh_ref, sc_ref, g_ref, w_ref, lg_ref, lb_ref, ws_ref, bs_ref,
                   q_ref, k_ref, kb_ref, v_ref, vb_ref, qi_ref, ki_ref, kib_ref, wi_ref, oa_ref, vn_ref,
                   *, tm, tq, chunked):
    x = x_ref[...]
    h = _rms(x, g_ref[...]) * (1.0 + sc_ref[...]) + sh_ref[...]
    p = jnp.dot(h.astype(BF16), w_ref[...], preferred_element_type=F32)
    k = p[:, _C_K:_C_VV]
    vv = p[:, _C_VV:_C_QI]
    ki = p[:, _C_KI:_C_KI + IDX_DIM]
    qs = p[:, _C_Q:_C_K] * (HEAD_DIM ** -0.5 * LOG2E)
    if chunked:
        qt = jnp.transpose(qs)
        hpg = MXU_DEPTH // HEAD_DIM
        row_head = lax.broadcasted_iota(I32, (MXU_DEPTH, 1), 0) // HEAD_DIM
        for pr in range(N_HEADS // 2):
            g = (2 * pr) // hpg
            qg = qt[g * MXU_DEPTH:(g + 1) * MXU_DEPTH, :]
            for half in range(2):
                qh = jnp.where(row_head == (2 * pr + half) % hpg, qg, 0.0).astype(BF16)
                for bb in range(tm // tq):
                    q_ref[pr, :, (2 * bb + half) * tq:(2 * bb + half + 1) * tq] = qh[:, bb * tq:(bb + 1) * tq]
    else:
        q_ref[...] = qs.astype(BF16)
    k_ref[...] = k
    kb_ref[...] = k.astype(BF16)
    v_ref[...] = vv
    vb_ref[...] = vv.astype(BF16)
    qi_ref[...] = (p[:, _C_QI:_C_KI] * (IDX_DIM ** -0.5)).astype(BF16)
    ki_ref[...] = ki
    kib_ref[...] = ki.astype(BF16)
    wi_ref[...] = p[:, _C_WI:_C_WI + IDX_HEADS] * (IDX_HEADS ** -0.5)
    gu = _gelu_tanh(p[:, _C_U:_C_V])
    gv = _gelu_tanh(p[:, _C_V:_C_Q])
    mu = jnp.mean(gv, axis=-1, keepdims=True)
    dv = gv - mu
    var = jnp.mean(dv * dv, axis=-1, keepdims=True)
    vn = dv * lax.rsqrt(var + EPS) * lg_ref[...] + lb_ref[...]
    vn_ref[...] = vn
    if chunked:
        gw = vn.shape[1] // GMLP_GROUPS
        vnb = vn.astype(BF16)
        for r in range(tm // CHUNK):
            rs = slice(r * CHUNK, (r + 1) * CHUNK)
            for g in range(GMLP_GROUPS):
                cs = slice(g * gw, (g + 1) * gw)
                s = jnp.dot(ws_ref[g], vnb[rs, cs], preferred_element_type=F32) + bs_ref[:, cs]
                oa_ref[rs, cs] = (gu[rs, cs] * s).astype(BF16)
    else:
        oa_ref[...] = (gu * (vn * ws_ref[...] + bs_ref[...])).astype(BF16)


def _premix(x, shift, scale, g_pre, w_in_p, ln_g, ln_b, ws, bs, *, tm, tq, chunked):
    t, d = x.shape
    per_row = shift.shape[0] != 1
    mod_spec = pl.BlockSpec((tm, d), lambda i: (i, 0)) if per_row else pl.BlockSpec((1, d), lambda i: (0, 0))
    const2 = lambda a: pl.BlockSpec(a.shape, lambda i: (0, 0))
    ws_spec = pl.BlockSpec(ws.shape, lambda i: (0, 0, 0)) if chunked else const2(ws)
    row = lambda n: pl.BlockSpec((tm, n), lambda i: (i, 0))
    outs = [((t, 512), BF16), ((t, 512), F32), ((t, 512), BF16), ((t, 512), F32), ((t, 512), BF16),
            ((t, 512), BF16), ((t, IDX_DIM), F32), ((t, IDX_DIM), BF16), ((t, IDX_HEADS), F32),
            ((t, 512), BF16), ((t, 512), F32)]
    out_specs = [row(s[1]) for s, _ in outs]
    if chunked:
        outs[0] = ((N_HEADS // 2, MXU_DEPTH, 2 * t), BF16)
        out_specs[0] = pl.BlockSpec((N_HEADS // 2, MXU_DEPTH, 2 * tm), lambda i: (0, 0, i))
    return pl.pallas_call(
        functools.partial(_premix_kernel, tm=tm, tq=tq, chunked=chunked),
        out_shape=[jax.ShapeDtypeStruct(s, dt) for s, dt in outs],
        grid=(t // tm,),
        in_specs=[row(d), mod_spec, mod_spec, const2(g_pre), const2(w_in_p), const2(ln_g), const2(ln_b),
                  ws_spec, const2(bs)],
        out_specs=out_specs,
        compiler_params=_cparams("arbitrary"),
        name="premix_chunked" if chunked else "premix_rows",
    )(x, shift, scale, g_pre, w_in_p, ln_g, ln_b, ws, bs)


_CNT_ROWS = 64
_RED_ROWS = 32


def _dsa_prompt_kernel(qz2_ref, qit_ref, wt_ref, kidx_ref, k_ref, vt_ref, o_ref,
                       s_ref, m_ref, l_ref, acc_ref, lg_ref, p_ref, *, tq, tk1, tk, tk3, topk, nbits):
    q0 = pl.program_id(0) * tq
    n_chunks = (q0 + tq + tk - 1) // tk
    n_chunks3 = (q0 + tq + tk3 - 1) // tk3
    qpos = q0 + lax.broadcasted_iota(I32, (1, tq), 1)
    w = wt_ref[...]

    def key_pos(c, size):
        return c * size + lax.broadcasted_iota(I32, (size, 1), 0)

    def p1(c, carry):
        k0 = pl.multiple_of(c * tk1, tk1)
        for r in range(tk1 // LANES):
            r0 = k0 + r * LANES
            kc = kidx_ref[pl.ds(r0, LANES), :]
            acc = jnp.zeros((LANES, tq), F32)
            for h in range(IDX_HEADS):
                d = jnp.dot(kc, qit_ref[h * IDX_DIM:(h + 1) * IDX_DIM, :], preferred_element_type=F32)
                acc = acc + w[h:h + 1, :] * jnp.maximum(d, 0.0)
            kp = r0 + lax.broadcasted_iota(I32, (LANES, 1), 0)
            s_ref[pl.ds(r0, LANES), :] = jnp.where(kp <= qpos, acc, NEG_INF)
        return carry

    lax.fori_loop(0, (q0 + tq + tk1 - 1) // tk1, p1, 0)

    def count(pred):
        def body(c, acc):
            k0 = pl.multiple_of(c * tk, tk)
            hit = jnp.where(pred(s_ref[pl.ds(k0, tk), :], c), 1.0, 0.0)
            return acc + jnp.sum(hit.reshape(tk // _CNT_ROWS, _CNT_ROWS, tq), axis=0)
        acc = lax.fori_loop(0, n_chunks, body, jnp.zeros((_CNT_ROWS, tq), F32))
        return jnp.sum(acc, axis=0, keepdims=True)

    kf = float(topk)
    select_all = (qpos + 1) <= topk
    cnt0 = count(lambda s, c: s >= 0.0)
    nonneg = cnt0 >= kf
    base0 = jnp.where(nonneg, jnp.int32(0), jnp.int32(INT_MIN))
    cntb0 = jnp.where(nonneg, cnt0, (qpos + 1).astype(F32))

    def bisect(b, state):
        base, cntb = state
        cand = base | jnp.left_shift(jnp.int32(1), 30 - b)
        thr = _ordered_bits_to_float(cand)
        c = count(lambda s, cc: s >= thr)
        ok = c >= kf
        return jnp.where(ok, cand, base), jnp.where(ok, c, cntb)

    base, cntb = lax.fori_loop(0, 31, bisect, (base0, cntb0))
    tau = jnp.where(select_all, NEG_INF, _ordered_bits_to_float(base))
    straddle = jnp.logical_and(cntb > kf, jnp.logical_not(select_all))

    def tie_search():
        need = kf - count(lambda s, c: s > tau)

        def step(b, j):
            cand = j | jnp.left_shift(jnp.int32(1), nbits - 1 - b)
            below = count(lambda s, c: jnp.logical_and(s == tau, key_pos(c, tk) < cand))
            return jnp.where(below < need, cand, j)
        return lax.fori_loop(0, nbits, step, jnp.zeros((1, tq), I32))

    any_straddle = jnp.max(jnp.where(straddle, 1, 0)) > 0
    j_tie = lax.cond(any_straddle, tie_search, lambda: jnp.zeros((1, tq), I32))
    j_tie = jnp.where(straddle, j_tie, jnp.int32(INT_MAX))

    m_ref[...] = jnp.full(m_ref.shape, NEG_INF, F32)
    l_ref[...] = jnp.zeros(l_ref.shape, F32)
    acc_ref[...] = jnp.zeros(acc_ref.shape, F32)
    hpg = MXU_DEPTH // HEAD_DIM
    col = lambda x, op: op(op(x.reshape(tk3 // _RED_ROWS, _RED_ROWS, tq), axis=0), axis=0, keepdims=True)

    def chunk_start(c):
        return pl.multiple_of(jnp.minimum(c, n_chunks3 - 1) * tk3, tk3)

    def logits(c, buf):
        k0 = chunk_start(c)
        s = s_ref[pl.ds(k0, tk3), :]
        kp = k0 + lax.broadcasted_iota(I32, (tk3, 1), 0)
        sel = jnp.logical_or(s > tau, jnp.logical_and(s == tau, kp <= j_tie))
        live = jnp.logical_and(kp <= qpos, c < n_chunks3)
        bias = jnp.where(jnp.logical_and(sel, live), 0.0, NEG_INF)
        cmax = []
        for pr in range(N_HEADS // 2):
            g = (2 * pr) // hpg
            kc = k_ref[pl.ds(k0, tk3), g * MXU_DEPTH:(g + 1) * MXU_DEPTH]
            lg2 = jnp.dot(kc, qz2_ref[pr], preferred_element_type=F32)
            for half in range(2):
                lg = lg2[:, half * tq:(half + 1) * tq] + bias
                lg_ref[buf, 2 * pr + half] = lg
                cmax.append(col(lg, jnp.max))
        return jnp.concatenate(cmax, axis=0)

    def softmax_pv(c, buf, cmax):
        k0 = chunk_start(c)
        m_old = m_ref[...]
        m_new = jnp.maximum(m_old, cmax)
        m_safe = jnp.where(m_new == NEG_INF, 0.0, m_new)
        alpha = jnp.exp2(m_old - m_safe)
        m_ref[...] = m_new
        psum = []
        for h in range(N_HEADS):
            p = jnp.exp2(lg_ref[buf, h] - m_safe[h:h + 1, :])
            p_ref[buf, h] = p.astype(BF16)
            psum.append(col(p, jnp.sum))
        l_ref[...] = alpha * l_ref[...] + jnp.concatenate(psum, axis=0)
        for h in range(N_HEADS):
            hs = slice(h * HEAD_DIM, (h + 1) * HEAD_DIM)
            pv = jnp.dot(vt_ref[hs, pl.ds(k0, tk3)], p_ref[buf, h], preferred_element_type=F32)
            acc_ref[hs, :] = alpha[h:h + 1, :] * acc_ref[hs, :] + pv

    def p3(j, cmax_even):
        cmax_odd = logits(2 * j + 1, 1)
        softmax_pv(2 * j, 0, cmax_even)
        cmax_even = logits(2 * j + 2, 0)
        softmax_pv(2 * j + 1, 1, cmax_odd)
        return cmax_even

    lax.fori_loop(0, (n_chunks3 + 1) // 2, p3, logits(0, 0))
    for h in range(N_HEADS):
        hs = slice(h * HEAD_DIM, (h + 1) * HEAD_DIM)
        acc_ref[hs, :] = acc_ref[hs, :] * (1.0 / l_ref[h:h + 1, :])
    o_ref[...] = jnp.transpose(acc_ref[...]).astype(BF16)


def _dsa_prompt(qz2, qit, wt, kidx_b, k_b, vt_b, *, tq, tk1, tk, tk3):
    t = k_b.shape[0]
    topk = min(TOPK_MAX, t // 4)
    nbits = max(1, math.ceil(math.log2(t)))
    resident = lambda a: pl.BlockSpec(a.shape, lambda i: (0,) * a.ndim, pipeline_mode=pl.Buffered(1))
    return pl.pallas_call(
        functools.partial(_dsa_prompt_kernel, tq=tq, tk1=tk1, tk=tk, tk3=tk3, topk=topk, nbits=nbits),
        out_shape=jax.ShapeDtypeStruct((t, N_HEADS * HEAD_DIM), BF16),
        grid=(t // tq,),
        in_specs=[pl.BlockSpec((N_HEADS // 2, MXU_DEPTH, 2 * tq), lambda i: (0, 0, i)),
                  pl.BlockSpec((IDX_HEADS * IDX_DIM, tq), lambda i: (0, i)),
                  pl.BlockSpec((IDX_HEADS, tq), lambda i: (0, i)),
                  resident(kidx_b), resident(k_b), resident(vt_b)],
        out_specs=pl.BlockSpec((tq, N_HEADS * HEAD_DIM), lambda i: (i, 0)),
        scratch_shapes=[pltpu.VMEM((t, tq), F32), pltpu.VMEM((N_HEADS, tq), F32),
                        pltpu.VMEM((N_HEADS, tq), F32), pltpu.VMEM((N_HEADS * HEAD_DIM, tq), F32),
                        pltpu.VMEM((2, N_HEADS, tk3, tq), F32), pltpu.VMEM((2, N_HEADS, tk3, tq), BF16)],
        compiler_params=_cparams("arbitrary"),
        name="dsa_prompt",
    )(qz2, qit, wt, kidx_b, k_b, vt_b)


def _idx_sample_kernel(pt_ref, q_ref, w_ref, kn_ref, *rest, pg):
    page_refs, s_ref, sn_ref = rest[:pg], rest[pg], rest[pg + 1]
    q = q_ref[...]
    w = w_ref[...]
    for r in range(pg):
        kp = page_refs[r][...].astype(BF16)
        d = jnp.dot(q, kp, preferred_element_type=F32)
        s_ref[r:r + 1, :] = jnp.sum(w * jnp.maximum(d, 0.0), axis=0, keepdims=True)
    kn = kn_ref[...].astype(BF16).astype(F32)
    dn = jnp.sum(q.astype(F32) * kn, axis=1, keepdims=True)
    sn = jnp.sum(w * jnp.maximum(dn, 0.0), axis=0, keepdims=True)
    sn_ref[...] = jnp.broadcast_to(sn, sn_ref.shape)


def _idx_sample(page_flat, qi_s, w_s, kidx_new, cache_kidx, layer, *, n_pages, pg):
    bd = qi_s.shape[0]
    page_spec = lambda r: pl.BlockSpec((None, None, IDX_DIM, PAGE_SIZE),
                                       lambda b, j, pt: (layer, pt[b * n_pages + j * pg + r], 0, 0))
    grid_spec = pltpu.PrefetchScalarGridSpec(
        num_scalar_prefetch=1,
        grid=(bd, n_pages // pg),
        in_specs=[pl.BlockSpec((None, IDX_HEADS, IDX_DIM), lambda b, j, pt: (b, 0, 0)),
                  pl.BlockSpec((None, IDX_HEADS, 1), lambda b, j, pt: (b, 0, 0)),
                  pl.BlockSpec((None, 1, IDX_DIM), lambda b, j, pt: (b, 0, 0))]
                 + [page_spec(r) for r in range(pg)],
        out_specs=[pl.BlockSpec((None, pg, PAGE_SIZE), lambda b, j, pt: (b, j, 0)),
                   pl.BlockSpec((None, 1, LANES), lambda b, j, pt: (b, 0, 0))],
    )
    return pl.pallas_call(
        functools.partial(_idx_sample_kernel, pg=pg),
        out_shape=[jax.ShapeDtypeStruct((bd, n_pages, PAGE_SIZE), F32),
                   jax.ShapeDtypeStruct((bd, 1, LANES), F32)],
        grid_spec=grid_spec,
        compiler_params=_cparams("arbitrary", "arbitrary"),
        name="idx_sample",
    )(page_flat, qi_s, w_s, kidx_new, *([cache_kidx] * pg))


def _topk_sample_kernel(s_ref, sn_ref, mask_ref, mnew_ref, *, topk, nbits, group):
    rows = range(group)
    s = [s_ref[g] for g in rows]
    sn = [sn_ref[g][:, 0:1] for g in rows]
    n_pg, width = s[0].shape
    n_past = n_pg * width
    kpos = lax.broadcasted_iota(I32, (n_pg, width), 0) * width + lax.broadcasted_iota(I32, (n_pg, width), 1)
    kf = float(topk)

    def count(pred_past, pred_new):
        c = jnp.sum(jnp.sum(jnp.where(pred_past, 1.0, 0.0), axis=0, keepdims=True), axis=1, keepdims=True)
        return c + jnp.where(pred_new, 1.0, 0.0)

    base0 = tuple(jnp.where(count(s[g] >= 0.0, sn[g] >= 0.0) >= kf, jnp.int32(0), jnp.int32(INT_MIN)) for g in rows)

    def bisect(b, base):
        out = []
        for g in rows:
            cand = base[g] | jnp.left_shift(jnp.int32(1), 30 - b)
            thr = _ordered_bits_to_float(cand)
            out.append(jnp.where(count(s[g] >= thr, sn[g] >= thr) >= kf, cand, base[g]))
        return tuple(out)

    tau = [_ordered_bits_to_float(u) for u in lax.fori_loop(0, 31, bisect, base0)]
    need = [kf - count(s[g] > tau[g], sn[g] > tau[g]) for g in rows]

    def step(b, j):
        out = []
        for g in rows:
            cand = j[g] | jnp.left_shift(jnp.int32(1), nbits - 1 - b)
            below = count(jnp.logical_and(s[g] == tau[g], kpos < cand),
                          jnp.logical_and(sn[g] == tau[g], n_past < cand))
            out.append(jnp.where(below < need[g], cand, j[g]))
        return tuple(out)

    j_tie = lax.fori_loop(0, nbits, step, tuple(jnp.zeros((1, 1), I32) for _ in rows))
    for g in rows:
        sel = jnp.logical_or(s[g] > tau[g], jnp.logical_and(s[g] == tau[g], kpos <= j_tie[g]))
        sel_new = jnp.logical_or(sn[g] > tau[g], jnp.logical_and(sn[g] == tau[g], n_past <= j_tie[g]))
        mask_ref[g] = jnp.where(sel, 1.0, 0.0)
        mnew_ref[g] = jnp.broadcast_to(jnp.where(sel_new, 1.0, 0.0), mnew_ref.shape[1:])


def _topk_sample(scores, snew, topk):
    bd, n_pages, _ = scores.shape
    nbits = max(1, math.ceil(math.log2(n_pages * PAGE_SIZE + 1)))
    group = 8 if bd % 8 == 0 else 1
    return pl.pallas_call(
        functools.partial(_topk_sample_kernel, topk=topk, nbits=nbits, group=group),
        out_shape=[jax.ShapeDtypeStruct(scores.shape, F32), jax.ShapeDtypeStruct(snew.shape, F32)],
        grid=(bd // group,),
        in_specs=[pl.BlockSpec((group, n_pages, PAGE_SIZE), lambda b: (b, 0, 0)),
                  pl.BlockSpec((group, 1, LANES), lambda b: (b, 0, 0))],
        out_specs=[pl.BlockSpec((group, n_pages, PAGE_SIZE), lambda b: (b, 0, 0)),
                   pl.BlockSpec((group, 1, LANES), lambda b: (b, 0, 0))],
        compiler_params=_cparams("arbitrary"),
        name="topk_sample",
    )(scores, snew)


def _attn_sample_kernel(pt_ref, qz_ref, kn_ref, vn_ref, mask_ref, mnew_ref, *rest, pg):
    k_refs, v_refs = rest[:pg], rest[pg:2 * pg]
    o_ref, m_ref, l_ref, acc_ref = rest[2 * pg:]
    j = pl.program_id(1)
    qz = qz_ref[...]
    d_attn = qz.shape[1]

    @pl.when(j == 0)
    def _():
        sel_new = mnew_ref[...][:, 0:1] > 0.0
        kn = kn_ref[...].astype(BF16).astype(F32)
        lg = jnp.sum(qz.astype(F32) * kn, axis=1, keepdims=True)
        m_ref[...] = jnp.where(sel_new, lg, NEG_INF)
        l_ref[...] = jnp.where(sel_new, jnp.ones_like(lg), 0.0)
        vn = vn_ref[...].astype(BF16).astype(F32)
        acc_ref[...] = jnp.where(sel_new, jnp.broadcast_to(vn, acc_ref.shape), 0.0)

    lgs = []
    for r in range(pg):
        lg = jnp.dot(qz, k_refs[r][...].astype(BF16), preferred_element_type=F32)
        lgs.append(jnp.where(mask_ref[r:r + 1, :] > 0.0, lg, NEG_INF))
    lg_max = lgs[0]
    for r in range(1, pg):
        lg_max = jnp.maximum(lg_max, lgs[r])
    m_old = m_ref[...]
    m_new = jnp.maximum(m_old, jnp.max(lg_max, axis=1, keepdims=True))
    m_safe = jnp.where(m_new == NEG_INF, 0.0, m_new)
    alpha = jnp.exp2(m_old - m_safe)
    p_sum = None
    pv = None
    for r in range(pg):
        p = jnp.exp2(lgs[r] - m_safe)
        p_sum = p if r == 0 else p_sum + p
        pv_r = lax.dot_general(p.astype(BF16), v_refs[r][...].astype(BF16), (((1,), (1,)), ((), ())),
                               preferred_element_type=F32)
        pv = pv_r if r == 0 else pv + pv_r
    l_ref[...] = alpha * l_ref[...] + jnp.sum(p_sum, axis=1, keepdims=True)
    acc_ref[...] = alpha * acc_ref[...] + pv
    m_ref[...] = m_new

    @pl.when(j == pl.num_programs(1) - 1)
    def _():
        head_of_lane = lax.broadcasted_iota(I32, (N_HEADS, d_attn), 1) // HEAD_DIM
        own = head_of_lane == lax.broadcasted_iota(I32, (N_HEADS, d_attn), 0)
        o = jnp.where(own, acc_ref[...] * (1.0 / l_ref[...]), 0.0)
        o_ref[...] = jnp.sum(o, axis=0, keepdims=True)


def _attn_sample(page_flat, qz_s, k_new, v_new, mask, mnew, cache_kt, cache_vt, layer, *, n_pages, pg):
    bd, _, d_attn = qz_s.shape
    page_spec = lambda r: pl.BlockSpec((None, None, d_attn, PAGE_SIZE),
                                       lambda b, j, pt: (layer, pt[b * n_pages + j * pg + r], 0, 0))
    per_b = lambda n, w: pl.BlockSpec((None, n, w), lambda b, j, pt: (b, 0, 0))
    grid_spec = pltpu.PrefetchScalarGridSpec(
        num_scalar_prefetch=1,
        grid=(bd, n_pages // pg),
        in_specs=[per_b(N_HEADS, d_attn), per_b(1, d_attn), per_b(1, d_attn),
                  pl.BlockSpec((None, pg, PAGE_SIZE), lambda b, j, pt: (b, j, 0)),
                  per_b(1, LANES)]
                 + [page_spec(r) for r in range(pg)] * 2,
        out_specs=per_b(1, d_attn),
        scratch_shapes=[pltpu.VMEM((N_HEADS, 1), F32), pltpu.VMEM((N_HEADS, 1), F32),
                        pltpu.VMEM((N_HEADS, d_attn), F32)],
    )
    return pl.pallas_call(
        functools.partial(_attn_sample_kernel, pg=pg),
        out_shape=jax.ShapeDtypeStruct((bd, 1, d_attn), F32),
        grid_spec=grid_spec,
        compiler_params=_cparams("arbitrary", "arbitrary"),
        name="attn_sample",
    )(page_flat, qz_s, k_new, v_new, mask, mnew, *([cache_kt] * pg), *([cache_vt] * pg))


def _postmix_kernel(oa_ref, ob_ref, woa_ref, wob_ref, x_ref, g1_ref, sh_ref, sc_ref, gpm_ref, gpf_ref,
                    wr_ref, br_ref, x1_ref, h2_ref, se_ref, sw_ref):
    mix = (jnp.dot(oa_ref[...], woa_ref[...], preferred_element_type=F32)
           + jnp.dot(ob_ref[...], wob_ref[...], preferred_element_type=F32))
    x1 = x_ref[...] + g1_ref[...] * _rms(mix, gpm_ref[...])
    x1_ref[...] = x1
    h2 = _rms(x1, gpf_ref[...]) * (1.0 + sc_ref[...]) + sh_ref[...]
    h2b = h2.astype(BF16)
    h2_ref[...] = h2
    logits = jnp.dot(h2b, wr_ref[...], preferred_element_type=F32) + br_ref[...]
    lane = lax.broadcasted_iota(I32, logits.shape, 1)
    lane_f = lane.astype(F32)
    se = jnp.zeros(logits.shape, F32)
    sw = jnp.zeros(logits.shape, F32)
    top = None
    denom = None
    for r in range(TOP_K_EXPERTS):
        m = jnp.max(logits, axis=1, keepdims=True)
        idx = jnp.min(jnp.where(logits == m, lane_f, float(LANES)), axis=1, keepdims=True)
        if r == 0:
            top = m
        e = jnp.exp(m - top)
        denom = e if r == 0 else denom + e
        se = jnp.where(lane == r, idx, se)
        sw = jnp.where(lane == r, e, sw)
        logits = jnp.where(lane_f == idx, NEG_INF, logits)
    se_ref[...] = se.astype(I32)
    sw_ref[...] = sw * (1.0 / denom)


def _postmix(out_a, out_b, wo_a, wo_b, x, gate1, shift2, scale2, g_pm, g_pf, wr_p, br_p, *, tm):
    t, d = x.shape
    per_row = gate1.shape[0] != 1
    mod_spec = pl.BlockSpec((tm, d), lambda i: (i, 0)) if per_row else pl.BlockSpec((1, d), lambda i: (0, 0))
    const2 = lambda a: pl.BlockSpec(a.shape, lambda i: (0, 0))
    row = lambda n: pl.BlockSpec((tm, n), lambda i: (i, 0))
    return pl.pallas_call(
        _postmix_kernel,
        out_shape=[jax.ShapeDtypeStruct((t, d), F32), jax.ShapeDtypeStruct((t, d), F32),
                   jax.ShapeDtypeStruct((t, LANES), I32), jax.ShapeDtypeStruct((t, LANES), F32)],
        grid=(t // tm,),
        in_specs=[row(out_a.shape[1]), row(out_b.shape[1]), const2(wo_a), const2(wo_b), row(d),
                  mod_spec, mod_spec, mod_spec, const2(g_pm), const2(g_pf), const2(wr_p), const2(br_p)],
        out_specs=[row(d), row(d), row(LANES), row(LANES)],
        compiler_params=_cparams("arbitrary"),
        name="postmix",
    )(out_a, out_b, wo_a, wo_b, x, gate1, shift2, scale2, g_pm, g_pf, wr_p, br_p)


def _moe_kernel(te_ref, nu_ref, x_ref, wgu_ref, bgu_ref, wd_ref, bd_ref, o_ref, wgu_b, wd_b, *, d_ff):
    i = pl.program_id(0)
    changed = jnp.logical_or(i == 0, te_ref[i] != te_ref[jnp.maximum(i - 1, 0)])

    @pl.when(changed)
    def _():
        rows = 128
        def cast(r, carry):
            r0 = pl.multiple_of(r * rows, rows)
            wgu_b[pl.ds(r0, rows), :] = wgu_ref[pl.ds(r0, rows), :].astype(BF16)
            wd_b[pl.ds(r0, rows), :] = wd_ref[pl.ds(r0, rows), :].astype(BF16)
            return carry
        lax.fori_loop(0, wgu_b.shape[0] // rows, cast, 0)

    @pl.when(i < nu_ref[0])
    def _():
        gu = jnp.dot(x_ref[...].astype(BF16), wgu_b[...], preferred_element_type=F32) + bgu_ref[...]
        gate = jnp.minimum(gu[:, :d_ff], SWIGLU_LIMIT)
        up = jnp.clip(gu[:, d_ff:], -SWIGLU_LIMIT, SWIGLU_LIMIT)
        a = (up + 1.0) * (gate * (1.0 / (1.0 + jnp.exp(-SWIGLU_ALPHA * gate))))
        o_ref[...] = jnp.dot(a.astype(BF16), wd_b[...], preferred_element_type=F32) + bd_ref[...]

    @pl.when(i >= nu_ref[0])
    def _():
        o_ref[...] = jnp.zeros(o_ref.shape, F32)


def _moe(tile_e, n_used, xs, w_gate_up, b_gate_up, w_down, b_down, *, tmoe):
    n_rows, d = xs.shape
    n_tiles = n_rows // tmoe
    d_ff = w_down.shape[1]
    assert w_gate_up.shape[1] == d and w_down.shape[1] == w_down.shape[2] == d
    grid_spec = pltpu.PrefetchScalarGridSpec(
        num_scalar_prefetch=2,
        grid=(n_tiles,),
        in_specs=[pl.BlockSpec((tmoe, d), lambda i, te, nu: (i, 0)),
                  pl.BlockSpec((None, d, 2 * d_ff), lambda i, te, nu: (te[i], 0, 0)),
                  pl.BlockSpec((None, 1, 2 * d_ff), lambda i, te, nu: (te[i], 0, 0)),
                  pl.BlockSpec((None, d_ff, d), lambda i, te, nu: (te[i], 0, 0)),
                  pl.BlockSpec((None, 1, d), lambda i, te, nu: (te[i], 0, 0))],
        out_specs=pl.BlockSpec((tmoe, d), lambda i, te, nu: (i, 0)),
        scratch_shapes=[pltpu.VMEM((d, 2 * d_ff), BF16), pltpu.VMEM((d_ff, d), BF16)],
    )
    return pl.pallas_call(
        functools.partial(_moe_kernel, d_ff=d_ff),
        out_shape=jax.ShapeDtypeStruct((n_rows, d), F32),
        grid_spec=grid_spec,
        compiler_params=_cparams("arbitrary"),
        name="moe",
    )(tile_e, n_used, xs, w_gate_up, b_gate_up.reshape(N_EXPERTS, 1, -1), w_down,
      b_down.reshape(N_EXPERTS, 1, -1))


def _final_kernel(y4_ref, sw_ref, x1_ref, g2_ref, gpf_ref, o_ref):
    sw = sw_ref[...]
    f = ((y4_ref[0] * sw[:, 0:1] + y4_ref[1] * sw[:, 1:2]) + (y4_ref[2] * sw[:, 2:3] + y4_ref[3] * sw[:, 3:4]))
    o_ref[...] = x1_ref[...] + g2_ref[...] * _rms(f, gpf_ref[...])


def _final(y4, sw, x1, gate2, g_post_ffn, *, tm):
    t, d = x1.shape
    per_row = gate2.shape[0] != 1
    mod_spec = pl.BlockSpec((tm, d), lambda i: (i, 0)) if per_row else pl.BlockSpec((1, d), lambda i: (0, 0))
    return pl.pallas_call(
        _final_kernel,
        out_shape=jax.ShapeDtypeStruct((t, d), F32),
        grid=(t // tm,),
        in_specs=[pl.BlockSpec((TOP_K_EXPERTS, tm, d), lambda i: (0, i, 0)),
                  pl.BlockSpec((tm, LANES), lambda i: (i, 0)),
                  pl.BlockSpec((tm, d), lambda i: (i, 0)), mod_spec,
                  pl.BlockSpec((1, d), lambda i: (0, 0))],
        out_specs=pl.BlockSpec((tm, d), lambda i: (i, 0)),
        compiler_params=_cparams("arbitrary"),
        name="final",
    )(y4, sw, x1, gate2, g_post_ffn)


def _route(sel_e, tmoe):
    n_tok = sel_e.shape[0]
    n_assign = n_tok * TOP_K_EXPERTS
    onehot = sel_e[:, :, None] == jnp.arange(N_EXPERTS, dtype=I32)[None, None, :]
    per_tok = jnp.sum(onehot.astype(I32), axis=1)
    before = jnp.cumsum(per_tok, axis=0) - per_tok
    counts = before[-1] + per_tok[-1]
    padded = (counts + tmoe - 1) // tmoe * tmoe
    pad_end = jnp.cumsum(padded)
    pad_start = pad_end - padded
    dest = jnp.sum(jnp.where(onehot, (before + pad_start[None, :])[:, None, :], 0), axis=2)
    n_tiles = -(-n_assign // tmoe) + N_EXPERTS
    n_rows = n_tiles * tmoe
    flat_tok = jnp.repeat(jnp.arange(n_tok, dtype=I32), TOP_K_EXPERTS)
    row_tok = jnp.full((n_rows,), n_tok, I32).at[dest.reshape(-1)].set(flat_tok)
    tile_start = jnp.arange(n_tiles, dtype=I32) * tmoe
    tile_e = jnp.sum((tile_start[:, None] >= pad_end[None, :]).astype(I32), axis=1)
    n_used = (pad_end[-1] // tmoe).astype(I32).reshape(1)
    last_e = jnp.max(jnp.where(counts > 0, jnp.arange(N_EXPERTS, dtype=I32), 0))
    tile_e = jnp.where(tile_start < pad_end[-1], jnp.minimum(tile_e, N_EXPERTS - 1), last_e).astype(I32)
    return row_tok, tile_e, n_used, dest


def _pad_cols(a, n):
    return jnp.concatenate([a, jnp.zeros(a.shape[:-1] + (n - a.shape[-1],), a.dtype)], axis=-1)


def kernel(x_prompt, x_sample, c_prompt, c_sample, cache_k, cache_v, cache_kidx, page_table, w_ada, b_ada,
           g_pre_mix, w_in, gmlp_ln_g, gmlp_ln_b, gmlp_w_s, gmlp_b_s, w_out, g_post_mix, g_pre_ffn, w_router,
           b_router, w_gate_up, b_gate_up, w_down, b_down, g_post_ffn):
    depth = w_ada.shape[0]
    assert depth == 1 and x_prompt.shape[0] == 1 and x_sample.shape[1] == 1
    _, t, d = x_prompt.shape
    bd = x_sample.shape[0]
    n_pages = page_table.shape[1]
    n_past = n_pages * PAGE_SIZE
    d_attn = N_HEADS * HEAD_DIM
    l = 0
    row2 = lambda a: a.reshape(1, -1)

    c_all = jnp.concatenate([c_prompt, c_sample], axis=0)
    m_pad = -(-c_all.shape[0] // SUBLANES) * SUBLANES
    c_all = jnp.concatenate([c_all, jnp.zeros((m_pad - c_all.shape[0], d), F32)], axis=0)
    mod = _adaln(c_all, w_ada[l], b_ada[l])
    mod_p = [mod[0:1, i * d:(i + 1) * d] for i in range(6)]
    mod_s = [mod[1:1 + bd, i * d:(i + 1) * d] for i in range(6)]

    w_in_l = w_in[l]
    w_in_p = jnp.concatenate([w_in_l[:, :_C_KI], _pad_cols(w_in_l[:, 3072:3136], LANES),
                              _pad_cols(w_in_l[:, 3136:3144], LANES)], axis=1).astype(BF16)
    tril = jnp.tril(jnp.ones((CHUNK, CHUNK), dtype=bool))
    ws_chunk = jnp.where(tril[None], gmlp_w_s[l], 0.0).astype(BF16)
    gw = 512 // GMLP_GROUPS
    bs_chunk = jnp.repeat(jnp.transpose(gmlp_b_s[l]), gw, axis=1)
    ws_row = jnp.repeat(gmlp_w_s[l][:, 0, 0], gw).reshape(1, -1)
    bs_row = jnp.repeat(gmlp_b_s[l][:, 0], gw).reshape(1, -1)
    wo = w_out[l].astype(BF16)
    wo_a, wo_b = wo[:512], wo[512:]
    wr_p = _pad_cols(w_router[l], LANES).astype(BF16)
    br_p = jnp.concatenate([b_router[l], jnp.full((LANES - N_EXPERTS,), NEG_INF, F32)]).reshape(1, LANES)
    g_pre, g_pm, g_pf, g_po = row2(g_pre_mix[l]), row2(g_post_mix[l]), row2(g_pre_ffn[l]), row2(g_post_ffn[l])
    ln_g, ln_b = row2(gmlp_ln_g[l]), row2(gmlp_ln_b[l])

    xp = x_prompt[0]
    tq = 128
    (qz2_p, k_p, kb_p, v_p, vb_p, qi_p, ki_p, kib_p, wi_p, oa_p, _) = _premix(
        xp, mod_p[0], mod_p[1], g_pre, w_in_p, ln_g, ln_b, ws_chunk, bs_chunk, tm=256, tq=tq, chunked=True)
    ob_p = _dsa_prompt(qz2_p, jnp.transpose(qi_p), jnp.transpose(wi_p), kib_p, kb_p, jnp.transpose(vb_p),
                       tq=tq, tk1=min(1024, t), tk=min(512, t), tk3=min(512, t))
    x1_p, h2_p, se_p, sw_p = _postmix(oa_p, ob_p, wo_a, wo_b, xp, mod_p[2], mod_p[3], mod_p[4], g_pm, g_pf,
                                      wr_p, br_p, tm=256)

    xs_ = x_sample[:, 0]
    (q_s, k_s, _, v_s, _, qi_s, ki_s, _, wi_s, oa_s, vn_s) = _premix(
        xs_, mod_s[0], mod_s[1], g_pre, w_in_p, ln_g, ln_b, ws_row, bs_row, tm=bd, tq=tq, chunked=False)
    page_flat = page_table.reshape(-1)
    pg = 32 if n_pages % 32 == 0 else (16 if n_pages % 16 == 0 else 1)
    kidx_t = jnp.transpose(cache_kidx, (0, 1, 3, 2))
    k_t = jnp.transpose(cache_k, (0, 1, 3, 4, 2)).reshape(depth, -1, d_attn, PAGE_SIZE)
    v_t = jnp.transpose(cache_v, (0, 1, 3, 4, 2)).reshape(depth, -1, d_attn, PAGE_SIZE)
    scores, snew = _idx_sample(page_flat, qi_s.reshape(bd, IDX_HEADS, IDX_DIM), wi_s.reshape(bd, IDX_HEADS, 1),
                               ki_s.reshape(bd, 1, IDX_DIM), kidx_t, l, n_pages=n_pages, pg=pg)
    topk_s = min(TOPK_MAX, (n_past + 1) // 4)
    mask, mnew = _topk_sample(scores, snew, topk_s)
    head_of_lane = jnp.arange(d_attn, dtype=I32) // HEAD_DIM
    qz_s = jnp.where(head_of_lane[None, None, :] == jnp.arange(N_HEADS, dtype=I32)[None, :, None],
                     q_s[:, None, :], jnp.zeros((), BF16))
    pga = 16 if n_pages % 16 == 0 else (8 if n_pages % 8 == 0 else 1)
    ob_s = _attn_sample(page_flat, qz_s, k_s.reshape(bd, 1, d_attn), v_s.reshape(bd, 1, d_attn), mask, mnew,
                        k_t, v_t, l, n_pages=n_pages, pg=pga)
    x1_s, h2_s, se_s, sw_s = _postmix(oa_s, ob_s.reshape(bd, d_attn).astype(BF16), wo_a, wo_b, xs_, mod_s[2],
                                      mod_s[3], mod_s[4], g_pm, g_pf, wr_p, br_p, tm=bd)

    tmoe = 256
    h2_all = jnp.concatenate([h2_p, h2_s, jnp.zeros((1, d), F32)], axis=0)
    sel_e = jnp.concatenate([se_p[:, :TOP_K_EXPERTS], se_s[:, :TOP_K_EXPERTS]], axis=0)
    row_tok, tile_e, n_used, dest = _route(sel_e, tmoe)
    ys = _moe(tile_e, n_used, h2_all[row_tok], w_gate_up[l], b_gate_up[l], w_down[l], b_down[l], tmoe=tmoe)
    y_p = _final(ys[jnp.transpose(dest[:t])], sw_p, x1_p, mod_p[5], g_po, tm=256)
    y_s = _final(ys[jnp.transpose(dest[t:])], sw_s, x1_s, mod_s[5], g_po, tm=bd)

    hs = (N_HEADS, HEAD_DIM)
    return (y_p[None], y_s[:, None],
            k_p.reshape(1, 1, t, *hs), v_p.reshape(1, 1, t, *hs), ki_p.reshape(1, 1, t, IDX_DIM),
            k_s.reshape(1, bd, 1, *hs), v_s.reshape(1, bd, 1, *hs), ki_s.reshape(1, bd, 1, IDX_DIM),
            vn_s.reshape(1, bd, 1, -1))
```
